```python
import math
import jax
import jax.numpy as jnp
from jax import lax
import numpy as np

D_MODEL = 2048
BATCH = 32
SEQ = 256
DEPTH = 2
DEC_BATCH = 8
DEC_SEQ = 2048
PAST_LEN = 512

F32 = jnp.float32
GRID_W = 64
N_EVEN = (DEPTH + 1) // 2
N_ODD = DEPTH // 2
EPS = 1e-6
CONV_W = 4
CONV_PAD = (2, 1)

D_A = D_MODEL
H_A = 16
BLK_A = D_A // H_A
LRU_C = 8.0

H_B = 8
DK_B = D_MODEL // 16
DV_B = 2 * DK_B
RET_CHUNK = 64
ROPE_BASE = 10000.0
N_FREQ = DK_B // 4

H_QK = 16
H_V = 32
DK_C = 128
DV_C = 128
GDN_CHUNK = 64
QKV_C = 2 * H_QK * DK_C + H_V * DV_C

N_EXPERTS = 16
EXPERT_FF = D_MODEL
EC_FACTOR = 2

IN0 = 2 * D_A + 2 * H_B * DK_B + 2 * H_B * DV_B
OUT0 = D_A + H_B * DV_B
IN1 = QKV_C + H_V * DV_C + 4 * H_V
OUT1 = H_V * DV_C

kernel_name = 'hybrid_diffusion_rglru_retnet_gdn_ecmoe_step'


def rms(x):
    xf = x.astype(F32)
    return xf * lax.rsqrt(jnp.mean(xf * xf, axis=-1, keepdims=True) + EPS)


def l2norm(x):
    return x * lax.rsqrt(jnp.sum(x * x, axis=-1, keepdims=True) + EPS)


def split_cols(t, sizes):
    out, start = [], 0
    for s in sizes:
        out.append(t[..., start:start + s])
        start += s
    return out


def adaln(x, g, shift, scale):
    return (rms(x) * g.astype(F32) * (1.0 + scale.astype(F32)) + shift.astype(F32)).astype(x.dtype)


def modulation(cond, w, b):
    m = jax.nn.silu(cond.reshape(-1, D_MODEL)) @ w + b
    return jnp.split(m[:, None, :], 6, axis=-1)


def dwconv(x, w):
    return lax.conv_general_dilated(x, w[:, None, :].astype(x.dtype), window_strides=(1,), padding=(CONV_PAD,),
                                    dimension_numbers=('NWC', 'WIO', 'NWC'), feature_group_count=x.shape[-1])


def rope_tables(rows):
    n = rows * GRID_W
    row = jnp.repeat(jnp.arange(rows, dtype=F32), GRID_W)
    col = (jnp.arange(n) % GRID_W).astype(F32)
    inv = ROPE_BASE ** (-jnp.arange(N_FREQ, dtype=F32) / N_FREQ)
    ang = jnp.concatenate([row[:, None] * inv, col[:, None] * inv], axis=-1)
    return jnp.cos(ang), jnp.sin(ang)


def apply_rope(x, cos, sin):
    half = DK_B // 2
    x1, x2 = x[..., :half], x[..., half:]
    return jnp.concatenate([x1 * cos - x2 * sin, x1 * sin + x2 * cos], axis=-1)


def linear_scan(a, b, h0):
    b = b.at[:, 0].add(a[:, 0] * h0)
    def combine(l, r):
        return l[0] * r[0], r[0] * l[1] + r[1]
    _, h = lax.associative_scan(combine, (a, b), axis=1)
    return h


def rglru_dir(x, h0, w_a, b_a, w_i, b_i, lam):
    bsz, n, _ = x.shape
    xf = x.astype(F32)
    xb = xf.reshape(bsz, n, H_A, BLK_A)
    r = jax.nn.sigmoid(jnp.einsum('bnhi,hij->bnhj', xb, w_a.astype(F32)).reshape(bsz, n, D_A) + b_a.astype(F32))
    i = jax.nn.sigmoid(jnp.einsum('bnhi,hij->bnhj', xb, w_i.astype(F32)).reshape(bsz, n, D_A) + b_i.astype(F32))
    log_a = -LRU_C * r * jax.nn.softplus(-lam.astype(F32))
    a = jnp.exp(log_a)
    inp = jnp.sqrt(-jnp.expm1(2.0 * log_a)) * (i * xf)
    h = linear_scan(a, inp, h0.astype(F32))
    return h, h[:, -1]


def retention_dir(q, k, v, r0, log_gamma):
    bsz, nh, n, dk = q.shape
    dv = v.shape[-1]
    nc = n // RET_CHUNK
    qc = q.reshape(bsz, nh, nc, RET_CHUNK, dk)
    kc = k.reshape(bsz, nh, nc, RET_CHUNK, dk)
    vc = v.reshape(bsz, nh, nc, RET_CHUNK, dv)
    pos = jnp.arange(RET_CHUNK, dtype=F32)
    lg = log_gamma[:, None]
    rel = pos[:, None] - pos[None, :]
    dmask = jnp.where(rel >= 0, jnp.exp(lg[:, :, None] * jnp.maximum(rel, 0.0)), 0.0)
    scores = jnp.einsum('bhnid,bhnjd->bhnij', qc, kc) * dmask[None, :, None]
    o_in = jnp.einsum('bhnij,bhnjv->bhniv', scores, vc)
    q_dec = qc * jnp.exp(lg * (pos + 1.0))[None, :, None, :, None]
    k_dec = kc * jnp.exp(lg * (RET_CHUNK - 1.0 - pos))[None, :, None, :, None]
    kv = jnp.einsum('bhnjd,bhnjv->nbhdv', k_dec, vc)
    g_c = jnp.exp(log_gamma * RET_CHUNK)[None, :, None, None]
    def step(r, kv_i):
        return r * g_c + kv_i, r
    r_last, r_prev = lax.scan(step, r0.astype(F32), kv)
    o_x = jnp.einsum('bhnid,nbhdv->bhniv', q_dec, r_prev)
    return (o_in + o_x).reshape(bsz, nh, n, dv), r_last


def gdn_dir(q, k, v, beta, g, s0):
    bsz, nh, n, dk = q.shape
    dv = v.shape[-1]
    nc = n // GDN_CHUNK
    def chunks(t):
        return t.reshape(bsz, nh, nc, GDN_CHUNK, *t.shape[3:])
    qc, kc, vc, bc = chunks(q), chunks(k), chunks(v), chunks(beta)
    gc = jnp.cumsum(chunks(g), axis=-1)
    incl = jnp.tril(jnp.ones((GDN_CHUNK, GDN_CHUNK), dtype=bool))
    strict = jnp.tril(jnp.ones((GDN_CHUNK, GDN_CHUNK), dtype=bool), -1)
    rel = gc[..., :, None] - gc[..., None, :]
    decay = jnp.where(incl, jnp.exp(jnp.where(incl, rel, 0.0)), 0.0)
    kb = kc * bc[..., None]
    a_low = jnp.where(strict, jnp.einsum('bhnid,bhnjd->bhnij', kb, kc) * decay, 0.0)
    eye = jnp.eye(GDN_CHUNK, dtype=F32)
    t_inv = lax.linalg.triangular_solve(eye + a_low, jnp.broadcast_to(eye, a_low.shape), left_side=True, lower=True)
    u = jnp.einsum('bhnij,bhnjv->bhniv', t_inv, vc * bc[..., None])
    w = jnp.einsum('bhnij,bhnjd->bhnid', t_inv, kb * jnp.exp(gc)[..., None])
    attn = jnp.where(incl, jnp.einsum('bhnid,bhnjd->bhnij', qc, kc) * decay, 0.0)
    q_dec = qc * jnp.exp(gc)[..., None]
    g_last = gc[..., -1]
    k_dec = kc * jnp.exp(g_last[..., None] - gc)[..., None]
    xs = tuple(jnp.moveaxis(t, 2, 0) for t in (q_dec, u, w, attn, g_last, k_dec))
    def step(s, inp):
        qd, u_i, w_i, at, gl, kd = inp
        v_new = u_i - jnp.einsum('bhck,bhkv->bhcv', w_i, s)
        o = jnp.einsum('bhck,bhkv->bhcv', qd, s) + jnp.einsum('bhcj,bhjv->bhcv', at, v_new)
        s = s * jnp.exp(gl)[..., None, None] + jnp.einsum('bhck,bhcv->bhkv', kd, v_new)
        return s, o
    s_last, o = lax.scan(step, s0.astype(F32), xs)
    return jnp.moveaxis(o, 0, 2).reshape(bsz, nh, n, dv), s_last


def mixer_even(h, w_in, conv_w, wa, ba, wi, bi, lam, dec_logit, w_out, s_lru, s_ret, rope):
    bsz, n, _ = h.shape
    gate_a, x_a, q, k, v, g = split_cols(h @ w_in, (D_A, D_A, H_B * DK_B, H_B * DK_B, H_B * DV_B, H_B * DV_B))
    xa = dwconv(x_a, conv_w)
    ha_f, sa_f = rglru_dir(xa, s_lru[:, 0], wa[0], ba[0], wi[0], bi[0], lam[0])
    ha_b, sa_b = rglru_dir(jnp.flip(xa, 1), s_lru[:, 1], wa[1], ba[1], wi[1], bi[1], lam[1])
    out_a = jax.nn.gelu(gate_a) * (ha_f + jnp.flip(ha_b, 1)).astype(h.dtype)
    def heads(t, d):
        return t.reshape(bsz, n, H_B, d).transpose(0, 2, 1, 3).astype(F32)
    qh = heads(q, DK_B) * (DK_B ** -0.5)
    kh = heads(k, DK_B)
    vh = heads(v, DV_B)
    if rope is not None:
        qh = apply_rope(qh, rope[0], rope[1])
        kh = apply_rope(kh, rope[0], rope[1])
    log_g = jax.nn.log_sigmoid(dec_logit.astype(F32))
    ob_f, sr_f = retention_dir(qh, kh, vh, s_ret[:, 0], log_g[0])
    ob_b, sr_b = retention_dir(jnp.flip(qh, 2), jnp.flip(kh, 2), jnp.flip(vh, 2), s_ret[:, 1], log_g[1])
    ob = rms(ob_f + jnp.flip(ob_b, 2)).transpose(0, 2, 1, 3).reshape(bsz, n, H_B * DV_B)
    out_b = jax.nn.silu(g) * ob.astype(h.dtype)
    y = jnp.concatenate([out_a, out_b], axis=-1) @ w_out
    return y, jnp.stack([sa_f, sa_b], axis=1), jnp.stack([sr_f, sr_b], axis=1)


def mixer_odd(h, w_in, conv_w, a_log, dt_bias, norm_g, w_out, s_gdn):
    bsz, n, _ = h.shape
    qkv, z, ab = split_cols(h @ w_in, (QKV_C, H_V * DV_C, 4 * H_V))
    qkv = jax.nn.silu(dwconv(qkv, conv_w)).astype(F32)
    q, k, v = split_cols(qkv, (H_QK * DK_C, H_QK * DK_C, H_V * DV_C))
    rep = H_V // H_QK
    q = jnp.repeat(l2norm(q.reshape(bsz, n, H_QK, DK_C)), rep, axis=2) * (DK_C ** -0.5)
    k = jnp.repeat(l2norm(k.reshape(bsz, n, H_QK, DK_C)), rep, axis=2)
    qh = q.transpose(0, 2, 1, 3)
    kh = k.transpose(0, 2, 1, 3)
    vh = v.reshape(bsz, n, H_V, DV_C).transpose(0, 2, 1, 3)
    ab = ab.astype(F32).reshape(bsz, n, 2, 2, H_V)
    beta = jax.nn.sigmoid(ab[:, :, :, 0]).transpose(0, 2, 3, 1)
    g = (-jnp.exp(a_log.astype(F32)) * jax.nn.softplus(ab[:, :, :, 1] + dt_bias.astype(F32))).transpose(0, 2, 3, 1)
    o_f, s_f = gdn_dir(qh, kh, vh, beta[:, 0], g[:, 0], s_gdn[:, 0])
    o_b, s_b = gdn_dir(jnp.flip(qh, 2), jnp.flip(kh, 2), jnp.flip(vh, 2),
                       jnp.flip(beta[:, 1], -1), jnp.flip(g[:, 1], -1), s_gdn[:, 1])
    o = (o_f + jnp.flip(o_b, 2)).transpose(0, 2, 1, 3)
    o = rms(o) * norm_g.astype(F32) * jax.nn.silu(z.reshape(bsz, n, H_V, DV_C).astype(F32))
    y = o.reshape(bsz, n, OUT1).astype(h.dtype) @ w_out
    return y, jnp.stack([s_f, s_b], axis=1)


def ec_moe(h, w_router, w1, w3, w2):
    bsz, n, d = h.shape
    cap = EC_FACTOR * n // N_EXPERTS
    aff = jax.nn.softmax((h @ w_router).astype(F32), axis=-1)
    gate, idx = lax.top_k(jnp.swapaxes(aff, 1, 2), cap)
    xg = jax.vmap(lambda hb, ib: hb[ib])(h, idx)
    hid = jax.nn.silu(jnp.einsum('becd,edf->becf', xg, w1)) * jnp.einsum('becd,edf->becf', xg, w3)
    y = jnp.einsum('becf,efd->becd', hid, w2) * gate[..., None].astype(h.dtype)
    return jax.vmap(lambda yb, ib: jnp.zeros((n, d), yb.dtype).at[ib.reshape(-1)].add(yb.reshape(-1, d)))(y, idx)


def setup_inputs(seed: int = 0) -> dict:
    key = jax.random.key(seed)
    ks = iter(jax.random.split(key, 40))
    def nrm(shape, scale):
        return jax.random.normal(next(ks), shape, F32) * scale
    D = D_MODEL
    x_prompt = nrm((BATCH, SEQ, D), 1.0)
    x_sample = nrm((DEC_BATCH, DEC_SEQ, D), 1.0)
    state_rglru = nrm((DEC_BATCH, N_EVEN, 2, D_A), 0.5)
    state_ret = nrm((DEC_BATCH, N_EVEN, 2, H_B, DK_B, DV_B), 0.5)
    state_gdn = nrm((DEC_BATCH, N_ODD, 2, H_V, DK_C, DV_C), 0.1)
    c = nrm((DEC_BATCH, D), 1.0)
    c_ctx = nrm((D,), 1.0)
    mod_w = nrm((DEPTH, D, 6 * D), 0.5 * D ** -0.5)
    mod_b = nrm((DEPTH, 6 * D), 0.02)
    norm1_g = 1.0 + nrm((DEPTH, D), 0.02)
    norm2_g = 1.0 + nrm((DEPTH, D), 0.02)
    w_in0 = nrm((N_EVEN, D, IN0), D ** -0.5)
    conv_a = nrm((N_EVEN, CONV_W, D_A), CONV_W ** -0.5)
    lru_wa = nrm((N_EVEN, 2, H_A, BLK_A, BLK_A), BLK_A ** -0.5)
    lru_ba = nrm((N_EVEN, 2, D_A), 0.02)
    lru_wi = nrm((N_EVEN, 2, H_A, BLK_A, BLK_A), BLK_A ** -0.5)
    lru_bi = nrm((N_EVEN, 2, D_A), 0.02)
    u = jax.random.uniform(next(ks), (N_EVEN, 2, D_A), F32, 0.9, 0.999)
    a0 = u ** (1.0 / LRU_C)
    lru_lam = jnp.log(a0) - jnp.log1p(-a0)
    gam = 1.0 - 2.0 ** (-5.0 - jnp.arange(H_B, dtype=F32))
    ret_decay = jnp.broadcast_to(jnp.log(gam) - jnp.log1p(-gam), (N_EVEN, 2, H_B)) + nrm((N_EVEN, 2, H_B), 0.1)
    w_out0 = nrm((N_EVEN, OUT0, D), OUT0 ** -0.5)
    w_in1 = nrm((N_ODD, D, IN1), D ** -0.5)
    conv_c = nrm((N_ODD, CONV_W, QKV_C), CONV_W ** -0.5)
    gdn_a_log = jnp.log(jax.random.uniform(next(ks), (N_ODD, 2, H_V), F32, 1.0, 16.0))
    dt = jnp.exp(jax.random.uniform(next(ks), (N_ODD, 2, H_V), F32, math.log(0.001), math.log(0.1)))
    gdn_dt_bias = dt + jnp.log(-jnp.expm1(-dt))
    gdn_norm_g = 1.0 + nrm((N_ODD, DV_C), 0.02)
    w_out1 = nrm((N_ODD, OUT1, D), OUT1 ** -0.5)
    router_w = nrm((DEPTH, D, N_EXPERTS), D ** -0.5)
    exp_w1 = nrm((DEPTH, N_EXPERTS, D, EXPERT_FF), D ** -0.5)
    exp_w3 = nrm((DEPTH, N_EXPERTS, D, EXPERT_FF), D ** -0.5)
    exp_w2 = nrm((DEPTH, N_EXPERTS, EXPERT_FF, D), EXPERT_FF ** -0.5)
    final_g = 1.0 + nrm((D,), 0.02)
    return {'x_prompt': x_prompt, 'x_sample': x_sample, 'state_rglru': state_rglru, 'state_ret': state_ret,
            'state_gdn': state_gdn, 'c': c, 'c_ctx': c_ctx, 'mod_w': mod_w, 'mod_b': mod_b,
            'norm1_g': norm1_g, 'norm2_g': norm2_g, 'w_in0': w_in0, 'conv_a': conv_a, 'lru_wa': lru_wa,
            'lru_ba': lru_ba, 'lru_wi': lru_wi, 'lru_bi': lru_bi, 'lru_lam': lru_lam, 'ret_decay': ret_decay,
            'w_out0': w_out0, 'w_in1': w_in1, 'conv_c': conv_c, 'gdn_a_log': gdn_a_log,
            'gdn_dt_bias': gdn_dt_bias, 'gdn_norm_g': gdn_norm_g, 'w_out1': w_out1, 'router_w': router_w,
            'exp_w1': exp_w1, 'exp_w3': exp_w3, 'exp_w2': exp_w2, 'final_g': final_g}


def reference(x_prompt, x_sample, state_rglru, state_ret, state_gdn, c, c_ctx, mod_w, mod_b, norm1_g, norm2_g,
              w_in0, conv_a, lru_wa, lru_ba, lru_wi, lru_bi, lru_lam, ret_decay, w_out0,
              w_in1, conv_c, gdn_a_log, gdn_dt_bias, gdn_norm_g, w_out1,
              router_w, exp_w1, exp_w3, exp_w2, final_g):
    rows = x_sample.shape[1] // GRID_W
    rope = rope_tables(rows)
    b_p = x_prompt.shape[0]
    zeros_lru = jnp.zeros((b_p, 2, D_A), F32)
    zeros_ret = jnp.zeros((b_p, 2, H_B, DK_B, DV_B), F32)
    zeros_gdn = jnp.zeros((b_p, 2, H_V, DK_C, DV_C), F32)
    xp, xs = x_prompt, x_sample
    out_lru, out_ret, out_gdn = [], [], []
    for l in range(DEPTH):
        mp = modulation(c_ctx, mod_w[l], mod_b[l])
        ms = modulation(c, mod_w[l], mod_b[l])
        hp = adaln(xp, norm1_g[l], mp[0], mp[1])
        hs = adaln(xs, norm1_g[l], ms[0], ms[1])
        if l % 2 == 0:
            e = l // 2
            yp, s_lru, s_ret = mixer_even(hp, w_in0[e], conv_a[e], lru_wa[e], lru_ba[e], lru_wi[e], lru_bi[e],
                                          lru_lam[e], ret_decay[e], w_out0[e], zeros_lru, zeros_ret, None)
            ys, _, _ = mixer_even(hs, w_in0[e], conv_a[e], lru_wa[e], lru_ba[e], lru_wi[e], lru_bi[e],
                                  lru_lam[e], ret_decay[e], w_out0[e], state_rglru[:, e], state_ret[:, e], rope)
            out_lru.append(s_lru)
            out_ret.append(s_ret)
        else:
            o = l // 2
            yp, s_gdn = mixer_odd(hp, w_in1[o], conv_c[o], gdn_a_log[o], gdn_dt_bias[o], gdn_norm_g[o],
                                  w_out1[o], zeros_gdn)
            ys, _ = mixer_odd(hs, w_in1[o], conv_c[o], gdn_a_log[o], gdn_dt_bias[o], gdn_norm_g[o],
                              w_out1[o], state_gdn[:, o])
            out_gdn.append(s_gdn)
        xp = xp + mp[2] * yp
        xs = xs + ms[2] * ys
        xp = xp + mp[5] * ec_moe(adaln(xp, norm2_g[l], mp[3], mp[4]), router_w[l], exp_w1[l], exp_w3[l], exp_w2[l])
        xs = xs + ms[5] * ec_moe(adaln(xs, norm2_g[l], ms[3], ms[4]), router_w[l], exp_w1[l], exp_w3[l], exp_w2[l])
    y_prompt = (rms(xp) * final_g.astype(F32)).astype(x_prompt.dtype)
    y_sample = (rms(xs) * final_g.astype(F32)).astype(x_sample.dtype)
    new_state_rglru = jnp.stack(out_lru, axis=1)
    new_state_ret = jnp.stack(out_ret, axis=1)
    new_state_gdn = jnp.stack(out_gdn, axis=1)
    return (y_prompt, y_sample, new_state_rglru, new_state_ret, new_state_gdn)
```

```python
import functools

import jax
import jax.numpy as jnp
from jax import lax
from jax.experimental import pallas as pl
from jax.experimental.pallas import tpu as pltpu

F32 = jnp.float32
BF16 = jnp.bfloat16
I32 = jnp.int32
ACT_DTYPE = BF16
EPS = 1e-6
GRID_W = 64
LRU_C = 8.0
ROPE_BASE = 10000.0
H_A, BLK_A = 16, 128
H_B, DK_B, DV_B = 8, 128, 256
H_QK, H_V, DK_C, DV_C = 16, 32, 128, 128
GDN_C = 64
RET_C = 256
N_EXPERTS = 16
EC_FACTOR = 2
V7X_VMEM_LIMIT = 56 * 1024 * 1024


def _cparams(n_axes):
    return pltpu.CompilerParams(dimension_semantics=("arbitrary",) * n_axes,
                                vmem_limit_bytes=V7X_VMEM_LIMIT)


def _silu(x):
    return x * jax.nn.sigmoid(x)


def _softplus(x):
    return jnp.maximum(x, 0.0) + jnp.log1p(jnp.exp(-jnp.abs(x)))


def _gelu_tanh(x):
    return 0.5 * x * (1.0 + jnp.tanh(0.7978845608028654 * (x + 0.044715 * (x * x * x))))


def _adaln(x, g, shift, scale):
    r = lax.rsqrt(jnp.mean(x * x, axis=-1, keepdims=True) + EPS)
    return x * r * g * (1.0 + scale) + shift


def _dot(a, b):
    return jnp.dot(a, b, preferred_element_type=F32)


def _dot_nt(a, b):
    return lax.dot_general(a, b, (((1,), (1,)), ((), ())), preferred_element_type=F32)


def _dot_tn(a, b):
    return lax.dot_general(a, b, (((0,), (0,)), ((), ())), preferred_element_type=F32)


def _mod_body(c_ref, w_ref, b_ref, o_ref):
    s = _silu(c_ref[...]).astype(BF16)
    o_ref[0] = _dot(s, w_ref[0].astype(BF16)) + b_ref[0]


def _modulation(cond, mod_w, mod_b):
    n_l, d, d6 = mod_w.shape
    r = cond.shape[0]
    tn = 1024
    out = pl.pallas_call(
        _mod_body, grid=(n_l, d6 // tn),
        in_specs=[pl.BlockSpec((r, d), lambda l, j: (0, 0)),
                  pl.BlockSpec((1, d, tn), lambda l, j: (l, 0, j)),
                  pl.BlockSpec((1, 1, tn), lambda l, j: (l, 0, j))],
        out_specs=pl.BlockSpec((1, r, tn), lambda l, j: (l, 0, j)),
        out_shape=jax.ShapeDtypeStruct((n_l, r, d6), F32),
        compiler_params=_cparams(2), name="modulation")(cond, mod_w, mod_b.reshape(n_l, 1, d6))
    return out.reshape(n_l, r, 6, d)


def _inproj_body(x_ref, m_ref, g_ref, w_ref, o_ref, h_scr, *, shift_idx, scale_idx):
    @pl.when(pl.program_id(1) == 0)
    def _():
        h = _adaln(x_ref[...], g_ref[...], m_ref[shift_idx:shift_idx + 1, :], m_ref[scale_idx:scale_idx + 1, :])
        h_scr[...] = h.astype(BF16)
    o_ref[...] = _dot(h_scr[...], w_ref[...]).astype(o_ref.dtype)


def _inproj(x, mod_l, row_fn, g, w, col0, ncols, out_dtype, tm, tn, shift_idx, scale_idx):
    t, d = x.shape
    c0 = col0 // tn
    return pl.pallas_call(
        functools.partial(_inproj_body, shift_idx=shift_idx, scale_idx=scale_idx),
        grid=(t // tm, ncols // tn),
        in_specs=[pl.BlockSpec((tm, d), lambda i, j: (i, 0)),
                  pl.BlockSpec((None, 6, d), lambda i, j: (row_fn(i), 0, 0)),
                  pl.BlockSpec((1, d), lambda i, j: (0, 0)),
                  pl.BlockSpec((d, tn), lambda i, j: (0, c0 + j))],
        out_specs=pl.BlockSpec((tm, tn), lambda i, j: (i, j)),
        out_shape=jax.ShapeDtypeStruct((t, ncols), out_dtype),
        scratch_shapes=[pltpu.VMEM((tm, d), BF16)],
        compiler_params=_cparams(2), name="adaln_inproj")(x, mod_l, g.reshape(1, d), w)


def _outproj_body(*refs, n_in, gate_idx):
    ins = refs[:n_in]
    w_ref, x_ref, m_ref, o_ref = refs[n_in:]
    acc = None
    k0 = 0
    for a in ins:
        kk = a.shape[1]
        part = _dot(a[...], w_ref[k0:k0 + kk, :])
        acc = part if acc is None else acc + part
        k0 += kk
    o_ref[...] = x_ref[...] + m_ref[gate_idx:gate_idx + 1, :] * acc


def _outproj(ins, w, x, mod_l, row_fn, tm, tn, gate_idx):
    t, d = x.shape
    ktot = w.shape[0]
    in_specs = [pl.BlockSpec((tm, a.shape[1]), lambda i, j: (i, 0)) for a in ins]
    in_specs += [pl.BlockSpec((ktot, tn), lambda i, j: (0, j)),
                 pl.BlockSpec((tm, tn), lambda i, j: (i, j)),
                 pl.BlockSpec((None, 6, tn), lambda i, j: (row_fn(i), 0, j))]
    return pl.pallas_call(
        functools.partial(_outproj_body, n_in=len(ins), gate_idx=gate_idx),
        grid=(t // tm, d // tn), in_specs=in_specs,
        out_specs=pl.BlockSpec((tm, tn), lambda i, j: (i, j)),
        out_shape=jax.ShapeDtypeStruct((t, d), F32),
        compiler_params=_cparams(2), name="outproj_residual")(*ins, w, x, mod_l)


def _fill_padded(xpad_ref, x_ref):
    n = x_ref.shape[0]
    zeros = jnp.zeros((8, xpad_ref.shape[1]), F32)
    xpad_ref[0:8, :] = zeros
    xpad_ref[n + 8:n + 16, :] = zeros
    xpad_ref[8:n + 8, :] = x_ref[...].astype(F32)


def _conv_rows(xpad_ref, w, s, rows):
    acc = None
    for k in range(4):
        term = w[k:k + 1, :] * xpad_ref[s + 6 + k:s + 6 + k + rows, :]
        acc = term if acc is None else acc + term
    return acc


def _lru_body(gate_ref, x_ref, cw_ref, wa_ref, wi_ref, ba_ref, bi_ref, lam_ref, h0_ref,
              out_ref, st_ref, xpad, a_f, b_f, a_b, b_b, *, rows):
    n, cw = x_ref.shape
    _fill_padded(xpad, x_ref)
    w = cw_ref[...]
    scr = ((a_f, b_f), (a_b, b_b))
    for s in range(0, n, rows):
        xa = _conv_rows(xpad, w, s, rows)
        xab = xa.astype(BF16)
        for d in range(2):
            r_parts, i_parts = [], []
            for i in range(cw // BLK_A):
                xb = xab[:, i * BLK_A:(i + 1) * BLK_A]
                r_parts.append(_dot(xb, wa_ref[d, i].astype(BF16)))
                i_parts.append(_dot(xb, wi_ref[d, i].astype(BF16)))
            r = jax.nn.sigmoid(jnp.concatenate(r_parts, axis=1) + ba_ref[d:d + 1, :])
            ig = jax.nn.sigmoid(jnp.concatenate(i_parts, axis=1) + bi_ref[d:d + 1, :])
            log_a = (-LRU_C) * r * _softplus(-lam_ref[d:d + 1, :])
            a = jnp.exp(log_a)
            mult = jnp.sqrt(jnp.tanh(-log_a) * (a * a + 1.0))
            scr[d][0][s:s + rows, :] = a
            scr[d][1][s:s + rows, :] = mult * (ig * xa)

    def step(t, carry):
        hf, hb = carry
        tb = n - 1 - t
        hf = a_f[pl.ds(t, 1), :] * hf + b_f[pl.ds(t, 1), :]
        hb = a_b[pl.ds(tb, 1), :] * hb + b_b[pl.ds(tb, 1), :]
        a_f[pl.ds(t, 1), :] = hf
        a_b[pl.ds(tb, 1), :] = hb
        return hf, hb

    hf, hb = lax.fori_loop(0, n, step, (h0_ref[0, 0:1, :], h0_ref[0, 1:2, :]), unroll=8)
    st_ref[0, 0:1, :] = hf
    st_ref[0, 1:2, :] = hb
    for s in range(0, n, rows):
        hsum = a_f[s:s + rows, :] + a_b[s:s + rows, :]
        out_ref[s:s + rows, :] = (_gelu_tanh(gate_ref[s:s + rows, :]) * hsum).astype(BF16)


def _rglru(z_a, bsz, n, conv_w, wa, wi, ba, bi, lam, h0):
    d_a = conv_w.shape[1]
    cw = 256
    nb = d_a // cw
    hb = cw // BLK_A
    rows = min(n, 256)
    return pl.pallas_call(
        functools.partial(_lru_body, rows=rows), grid=(bsz, nb),
        in_specs=[pl.BlockSpec((n, cw), lambda b, j: (b, j)),
                  pl.BlockSpec((n, cw), lambda b, j: (b, nb + j)),
                  pl.BlockSpec((4, cw), lambda b, j: (0, j)),
                  pl.BlockSpec((2, hb, BLK_A, BLK_A), lambda b, j: (0, j, 0, 0)),
                  pl.BlockSpec((2, hb, BLK_A, BLK_A), lambda b, j: (0, j, 0, 0)),
                  pl.BlockSpec((2, cw), lambda b, j: (0, j)),
                  pl.BlockSpec((2, cw), lambda b, j: (0, j)),
                  pl.BlockSpec((2, cw), lambda b, j: (0, j)),
                  pl.BlockSpec((1, 2, cw), lambda b, j: (b, 0, j))],
        out_specs=[pl.BlockSpec((n, cw), lambda b, j: (b, j)),
                   pl.BlockSpec((1, 2, cw), lambda b, j: (b, 0, j))],
        out_shape=[jax.ShapeDtypeStruct((bsz * n, d_a), BF16),
                   jax.ShapeDtypeStruct((bsz, 2, d_a), F32)],
        scratch_shapes=[pltpu.VMEM((n + 16, cw), F32)] + [pltpu.VMEM((n, cw), F32)] * 4,
        compiler_params=_cparams(2), name="rglru")(z_a, z_a, conv_w, wa, wi, ba, bi, lam, h0)


def _ret_body(*refs, use_rope, has_state, chunk):
    q_ref, k_ref, v_ref, g_ref, dec_ref = refs[:5]
    pos = 5
    if use_rope:
        cos_ref, sin_ref = refs[pos:pos + 2]
        pos += 2
    if has_state:
        s0_ref = refs[pos]
        pos += 1
    o_ref, st_ref, rb_scr = refs[pos:pos + 3]
    n = q_ref.shape[0]
    c_len = chunk
    nc = n // c_len
    head = pl.program_id(1)

    lg = -_softplus(-dec_ref[...])
    lane = lax.broadcasted_iota(I32, lg.shape, 1)
    lgh = jnp.sum(jnp.where(lane == head, lg, 0.0), axis=1, keepdims=True)
    lgf, lgb = lgh[0:1, :], lgh[1:2, :]

    def prep(ref, c, sc):
        x = ref[c * c_len:(c + 1) * c_len, :].astype(F32) * sc
        if use_rope:
            x = (x * cos_ref[c * c_len:(c + 1) * c_len, :]
                 + pltpu.roll(x, DK_B // 2, 1) * sin_ref[c * c_len:(c + 1) * c_len, :])
        return x

    ii = lax.broadcasted_iota(I32, (c_len, 1), 0).astype(F32)
    jj = lax.broadcasted_iota(I32, (1, c_len), 1).astype(F32)
    rel = ii - jj
    dmask = jnp.where(rel > 0.0, jnp.exp(lgf * jnp.maximum(rel, 0.0)),
                      jnp.where(rel < 0.0, jnp.exp(lgb * jnp.maximum(-rel, 0.0)), 2.0))
    qf_dec = jnp.exp(lgf * (ii + 1.0))
    qb_dec = jnp.exp(lgb * (c_len - ii))
    kf_dec = jnp.exp(lgf * (c_len - 1.0 - ii))
    kb_dec = jnp.exp(lgb * ii)
    gf_c = jnp.exp(lgf * c_len)
    gb_c = jnp.exp(lgb * c_len)

    r_b = s0_ref[0, 1, 0] if has_state else jnp.zeros((DK_B, DV_B), F32)
    for c in range(nc - 1, -1, -1):
        rb_scr[c] = r_b
        k = prep(k_ref, c, 1.0)
        v = v_ref[c * c_len:(c + 1) * c_len, :].astype(BF16)
        r_b = r_b * gb_c + _dot_tn((k * kb_dec).astype(BF16), v)
    st_ref[0, 1, 0] = r_b

    r_f = s0_ref[0, 0, 0] if has_state else jnp.zeros((DK_B, DV_B), F32)
    for c in range(nc):
        q = prep(q_ref, c, DK_B ** -0.5)
        k = prep(k_ref, c, 1.0)
        v = v_ref[c * c_len:(c + 1) * c_len, :].astype(BF16)
        s = _dot_nt(q.astype(BF16), k.astype(BF16)) * dmask
        o = _dot(s.astype(BF16), v)
        qd = jnp.concatenate([q * qf_dec, q * qb_dec], axis=1).astype(BF16)
        rcat = jnp.concatenate([r_f, rb_scr[c]], axis=0).astype(BF16)
        o = o + _dot(qd, rcat)
        r_f = r_f * gf_c + _dot_tn((k * kf_dec).astype(BF16), v)
        on = o * lax.rsqrt(jnp.mean(o * o, axis=-1, keepdims=True) + EPS)
        g = g_ref[c * c_len:(c + 1) * c_len, :].astype(F32)
        o_ref[c * c_len:(c + 1) * c_len, :] = (_silu(g) * on).astype(BF16)
    st_ref[0, 0, 0] = r_f


def _retention(z_b, bsz, n, dec, rope, s0):
    nq = H_B
    chunk = min(n, RET_C)
    in_specs = [pl.BlockSpec((n, DK_B), lambda b, h: (b, h)),
                pl.BlockSpec((n, DK_B), lambda b, h: (b, nq + h)),
                pl.BlockSpec((n, DV_B), lambda b, h: (b, nq + h)),
                pl.BlockSpec((n, DV_B), lambda b, h: (b, 2 * nq + h)),
                pl.BlockSpec((2, H_B), lambda b, h: (0, 0))]
    args = [z_b, z_b, z_b, z_b, dec]
    if rope is not None:
        in_specs += [pl.BlockSpec((n, DK_B), lambda b, h: (0, 0))] * 2
        args += list(rope)
    if s0 is not None:
        in_specs.append(pl.BlockSpec((1, 2, 1, DK_B, DV_B), lambda b, h: (b, 0, h, 0, 0)))
        args.append(s0)
    return pl.pallas_call(
        functools.partial(_ret_body, use_rope=rope is not None, has_state=s0 is not None, chunk=chunk),
        grid=(bsz, H_B), in_specs=in_specs,
        out_specs=[pl.BlockSpec((n, DV_B), lambda b, h: (b, h)),
                   pl.BlockSpec((1, 2, 1, DK_B, DV_B), lambda b, h: (b, 0, h, 0, 0))],
        out_shape=[jax.ShapeDtypeStruct((bsz * n, H_B * DV_B), BF16),
                   jax.ShapeDtypeStruct((bsz, 2, H_B, DK_B, DV_B), F32)],
        scratch_shapes=[pltpu.VMEM((n // chunk, DK_B, DV_B), F32)],
        compiler_params=_cparams(2), name="retention")(*args)


def _rope_tables(n):
    rows = n // GRID_W
    n_freq = DK_B // 4
    row = jnp.repeat(jnp.arange(rows, dtype=F32), GRID_W)
    col = (jnp.arange(n) % GRID_W).astype(F32)
    inv = ROPE_BASE ** (-jnp.arange(n_freq, dtype=F32) / n_freq)
    ang = jnp.concatenate([row[:, None] * inv, col[:, None] * inv], axis=-1)
    cos, sin = jnp.cos(ang), jnp.sin(ang)
    return jnp.concatenate([cos, cos], axis=-1), jnp.concatenate([-sin, sin], axis=-1)


def _gdn_body(*refs, has_state, rows):
    (zq_ref, zk_ref, zv_ref, zg_ref, cq_ref, ck_ref, cv_ref, ab_ref, abt_ref,
     a8_ref, dt8_ref, a8t_ref, dt8t_ref, ng_ref) = refs[:14]
    pos = 14
    if has_state:
        s0_ref = refs[pos]
        pos += 1
    (o_ref, st_ref, xpad, xpad_v, q_scr, k_scr, v_scr, o_scr, bg_scr, grow_scr, s_scr,
     u_scr, w_scr, at_scr, qd_scr, kd_scr, eg_scr) = refs[pos:]
    n = zq_ref.shape[0]
    c_len = GDN_C
    nc = n // c_len

    _fill_padded(xpad, zq_ref)
    for s in range(0, n, rows):
        q = _silu(_conv_rows(xpad, cq_ref[...], s, rows))
        q = q * lax.rsqrt(jnp.sum(q * q, axis=-1, keepdims=True) + EPS) * (DK_C ** -0.5)
        q_scr[s:s + rows, :] = q.astype(BF16)
    _fill_padded(xpad, zk_ref)
    for s in range(0, n, rows):
        k = _silu(_conv_rows(xpad, ck_ref[...], s, rows))
        k = k * lax.rsqrt(jnp.sum(k * k, axis=-1, keepdims=True) + EPS)
        k_scr[s:s + rows, :] = k.astype(BF16)
    _fill_padded(xpad_v, zv_ref)
    for s in range(0, n, rows):
        v_scr[s:s + rows, :] = _silu(_conv_rows(xpad_v, cv_ref[...], s, rows))

    ab = ab_ref[0, 0]
    gcols = -jnp.exp(a8_ref[0]) * _softplus(ab + dt8_ref[0])
    w_is_beta = ((lax.broadcasted_iota(I32, ab.shape, 1) >> 1) & 1) == 0
    bg_scr[...] = jnp.where(w_is_beta, jax.nn.sigmoid(ab), gcols)
    grow_scr[...] = -jnp.exp(a8t_ref[0]) * _softplus(abt_ref[0, 0] + dt8t_ref[0])
    o_scr[...] = jnp.zeros(o_scr.shape, F32)
    for d in range(2):
        for r in range(2):
            s_scr[2 * d + r] = s0_ref[0, d, r] if has_state else jnp.zeros((DK_C, DV_C), F32)

    ii = lax.broadcasted_iota(I32, (c_len, c_len), 0)
    jj = lax.broadcasted_iota(I32, (c_len, c_len), 1)
    eye = jnp.where(ii == jj, 1.0, 0.0)

    same8 = (ii >> 3) == (jj >> 3)
    merges = []
    sh = 3
    while (1 << sh) < c_len:
        merges.append(((ii >> (sh + 1)) == (jj >> (sh + 1))) & ((ii >> sh) != (jj >> sh)))
        sh += 1

    def tri_inverse(a_mat):
        a_pow = jnp.where(same8, a_mat, 0.0)
        p = eye - a_pow
        for _ in range(2):
            apb = a_pow.astype(BF16)
            a_pow = _dot(apb, apb)
            p = p + _dot(p.astype(BF16), a_pow.astype(BF16))
        for off in merges:
            pb = p.astype(BF16)
            p = p - _dot(pb, _dot(jnp.where(off, a_mat, 0.0).astype(BF16), pb).astype(BF16))
        return p

    def prep_chunk(c, carry):
        rows_c = pl.ds(pl.multiple_of(c * c_len, c_len), c_len)
        kc = k_scr[rows_c, :]
        qc = q_scr[rows_c, :]
        kcf = kc.astype(F32)
        qcf = qc.astype(F32)
        kk = _dot_nt(kc, kc)
        qk = _dot_nt(qc, kc)
        for d in range(2):
            incl = (jj <= ii) if d == 0 else (jj >= ii)
            strict = (jj < ii) if d == 0 else (jj > ii)
            incl_t = (ii <= jj) if d == 0 else (ii >= jj)
            for r in range(2):
                si = 2 * d + r
                beta = bg_scr[rows_c, 4 * d + r:4 * d + r + 1]
                gcol = bg_scr[rows_c, 4 * d + 2 + r:4 * d + 3 + r]
                grow = grow_scr[pl.ds((4 * d + 2 + r) * nc + c, 1), :]
                gc_col = jnp.sum(jnp.where(incl, grow, 0.0), axis=1, keepdims=True)
                gc_row = jnp.sum(jnp.where(incl_t, gcol, 0.0), axis=0, keepdims=True)
                decay = jnp.where(incl, jnp.exp(jnp.where(incl, gc_col - gc_row, 0.0)), 0.0)
                t_inv = tri_inverse(jnp.where(strict, beta * kk * decay, 0.0))
                egc = jnp.exp(gc_col)
                vc = v_scr[rows_c, r * DV_C:(r + 1) * DV_C]
                rhs = jnp.concatenate([vc * beta, kcf * (beta * egc)], axis=1).astype(BF16)
                uw = _dot(t_inv.astype(BF16), rhs)
                u_scr[si, rows_c, :] = uw[:, :DV_C]
                w_scr[si, rows_c, :] = uw[:, DV_C:].astype(BF16)
                at_scr[si, rows_c, :] = jnp.where(incl, qk * decay, 0.0).astype(BF16)
                qd_scr[si, rows_c, :] = (qcf * egc).astype(BF16)
                g_last = gc_col[c_len - 1:c_len, :] if d == 0 else gc_col[0:1, :]
                kd_scr[si, rows_c, :] = (kcf * jnp.exp(g_last - gc_col)).astype(BF16)
                eg_scr[pl.ds(si * nc + c, 1), :] = jnp.broadcast_to(jnp.exp(g_last), (1, DV_C))
        return carry

    lax.fori_loop(0, nc, prep_chunk, 0)

    def scan_chunk(c, carry):
        for d in range(2):
            cc = c if d == 0 else nc - 1 - c
            rows_c = pl.ds(pl.multiple_of(cc * c_len, c_len), c_len)
            for r in range(2):
                si = 2 * d + r
                s_mat = s_scr[si]
                sb = s_mat.astype(BF16)
                v_new = u_scr[si, rows_c, :] - _dot(w_scr[si, rows_c, :], sb)
                vnb = v_new.astype(BF16)
                o = _dot(qd_scr[si, rows_c, :], sb) + _dot(at_scr[si, rows_c, :], vnb)
                s_scr[si] = s_mat * eg_scr[pl.ds(si * nc + cc, 1), :] + _dot_tn(kd_scr[si, rows_c, :], vnb)
                o_scr[rows_c, r * DV_C:(r + 1) * DV_C] += o
        return carry

    lax.fori_loop(0, nc, scan_chunk, 0)

    for d in range(2):
        for r in range(2):
            st_ref[0, d, r] = s_scr[2 * d + r]
    ng = ng_ref[...]
    for s in range(0, n, rows):
        zg = zg_ref[s:s + rows, :].astype(F32)
        for r in range(2):
            o = o_scr[s:s + rows, r * DV_C:(r + 1) * DV_C]
            on = o * lax.rsqrt(jnp.mean(o * o, axis=-1, keepdims=True) + EPS) * ng
            o_ref[s:s + rows, r * DV_C:(r + 1) * DV_C] = (
                on * _silu(zg[:, r * DV_C:(r + 1) * DV_C])).astype(BF16)


def _gdn(z1, ab, bsz, n, conv_c, a_log, dt_bias, norm_g, s0):
    nc = n // GDN_C
    rows = min(n, 256)
    nqb = H_QK
    ab6 = ab.reshape(bsz, n, 2, 2, H_QK, 2)
    ab8 = ab6.transpose(0, 4, 1, 2, 3, 5).reshape(bsz, H_QK, n, 8)
    abt = ab6.transpose(0, 4, 2, 3, 5, 1).reshape(bsz, H_QK, 8 * nc, GDN_C)

    def per_pair(p):
        p3 = p.astype(F32).reshape(2, H_QK, 2).transpose(1, 0, 2)
        return jnp.broadcast_to(p3[:, :, None, :], (H_QK, 2, 2, 2)).reshape(H_QK, 8)

    a8, dt8 = per_pair(a_log), per_pair(dt_bias)
    in_specs = [pl.BlockSpec((n, DK_C), lambda b, j: (b, j)),
                pl.BlockSpec((n, DK_C), lambda b, j: (b, nqb + j)),
                pl.BlockSpec((n, 2 * DV_C), lambda b, j: (b, nqb + j)),
                pl.BlockSpec((n, 2 * DV_C), lambda b, j: (b, 2 * nqb + j)),
                pl.BlockSpec((4, DK_C), lambda b, j: (0, j)),
                pl.BlockSpec((4, DK_C), lambda b, j: (0, nqb + j)),
                pl.BlockSpec((4, 2 * DV_C), lambda b, j: (0, nqb + j)),
                pl.BlockSpec((1, 1, n, 8), lambda b, j: (b, j, 0, 0)),
                pl.BlockSpec((1, 1, 8 * nc, GDN_C), lambda b, j: (b, j, 0, 0)),
                pl.BlockSpec((1, 1, 8), lambda b, j: (j, 0, 0)),
                pl.BlockSpec((1, 1, 8), lambda b, j: (j, 0, 0)),
                pl.BlockSpec((1, 8 * nc, 1), lambda b, j: (j, 0, 0)),
                pl.BlockSpec((1, 8 * nc, 1), lambda b, j: (j, 0, 0)),
                pl.BlockSpec((1, DV_C), lambda b, j: (0, 0))]
    args = [z1, z1, z1, z1, conv_c, conv_c, conv_c, ab8, abt,
            a8.reshape(H_QK, 1, 8), dt8.reshape(H_QK, 1, 8),
            jnp.repeat(a8, nc, axis=1).reshape(H_QK, 8 * nc, 1),
            jnp.repeat(dt8, nc, axis=1).reshape(H_QK, 8 * nc, 1), norm_g.reshape(1, DV_C)]
    if s0 is not None:
        in_specs.append(pl.BlockSpec((1, 2, 2, DK_C, DV_C), lambda b, j: (b, 0, j, 0, 0)))
        args.append(s0)
    return pl.pallas_call(
        functools.partial(_gdn_body, has_state=s0 is not None, rows=rows),
        grid=(bsz, H_QK), in_specs=in_specs,
        out_specs=[pl.BlockSpec((n, 2 * DV_C), lambda b, j: (b, j)),
                   pl.BlockSpec((1, 2, 2, DK_C, DV_C), lambda b, j: (b, 0, j, 0, 0))],
        out_shape=[jax.ShapeDtypeStruct((bsz * n, H_V * DV_C), BF16),
                   jax.ShapeDtypeStruct((bsz, 2, H_V, DK_C, DV_C), F32)],
        scratch_shapes=[pltpu.VMEM((n + 16, DK_C), F32), pltpu.VMEM((n + 16, 2 * DV_C), F32),
                        pltpu.VMEM((n, DK_C), BF16), pltpu.VMEM((n, DK_C), BF16),
                        pltpu.VMEM((n, 2 * DV_C), F32), pltpu.VMEM((n, 2 * DV_C), F32),
                        pltpu.VMEM((n, 8), F32), pltpu.VMEM((8 * nc, GDN_C), F32),
                        pltpu.VMEM((4, DK_C, DV_C), F32),
                        pltpu.VMEM((4, n, DV_C), F32), pltpu.VMEM((4, n, DK_C), BF16),
                        pltpu.VMEM((4, n, GDN_C), BF16), pltpu.VMEM((4, n, DK_C), BF16),
                        pltpu.VMEM((4, n, DK_C), BF16), pltpu.VMEM((4 * nc, DV_C), F32)],
        compiler_params=_cparams(2), name="gated_deltanet")(*args)


def _router_body(x_ref, m_ref, g_ref, rwt_ref, h_ref, aff_ref):
    h = _adaln(x_ref[...], g_ref[...], m_ref[3:4, :], m_ref[4:5, :])
    h_ref[...] = h.astype(BF16)
    logits = _dot_nt(rwt_ref[...].astype(BF16), h.astype(BF16))
    ex = jnp.exp(logits - jnp.max(logits, axis=0, keepdims=True))
    aff_ref[0] = ex / jnp.sum(ex, axis=0, keepdims=True)


def _router(x, mod_l, row_fn, g, rw_t, bsz, n, tt):
    t, d = x.shape
    e = rw_t.shape[0]
    nt = n // tt
    return pl.pallas_call(
        _router_body, grid=(bsz, nt),
        in_specs=[pl.BlockSpec((tt, d), lambda b, i: (b * nt + i, 0)),
                  pl.BlockSpec((None, 6, d), lambda b, i: (row_fn(b), 0, 0)),
                  pl.BlockSpec((1, d), lambda b, i: (0, 0)),
                  pl.BlockSpec((e, d), lambda b, i: (0, 0))],
        out_specs=[pl.BlockSpec((tt, d), lambda b, i: (b * nt + i, 0)),
                   pl.BlockSpec((1, e, tt), lambda b, i: (b, 0, i))],
        out_shape=[jax.ShapeDtypeStruct((t, d), BF16), jax.ShapeDtypeStruct((bsz, e, n), F32)],
        compiler_params=_cparams(2), name="adaln_router")(x, mod_l, g.reshape(1, d), rw_t)


def _cumsum_lanes(m):
    r, n = m.shape
    blk = min(n, 256)
    tri = jnp.where(lax.broadcasted_iota(I32, (blk, blk), 0) <= lax.broadcasted_iota(I32, (blk, blk), 1),
                    1.0, 0.0).astype(BF16)
    outs = []
    run = jnp.zeros((r, 1), F32)
    for c in range(n // blk):
        loc = _dot(m[:, c * blk:(c + 1) * blk], tri) + run
        outs.append(loc)
        run = loc[:, blk - 1:blk]
    return outs[0] if len(outs) == 1 else jnp.concatenate(outs, axis=1)


def _topk_body(aff_ref, pos_ref, *, cap):
    bb, e, n = aff_ref.shape
    rows = bb * e
    bits = pltpu.bitcast(aff_ref[...].reshape(rows, n), I32)
    capf = float(cap)
    prefix = jnp.zeros((rows, 1), I32)
    for bit in range(30, -1, -1):
        cand = prefix | (1 << bit)
        cnt = jnp.sum(jnp.where(bits >= cand, 1.0, 0.0), axis=1, keepdims=True)
        prefix = jnp.where(cnt >= capf, cand, prefix)
    gt = bits > prefix
    eq = bits == prefix
    need = capf - jnp.sum(jnp.where(gt, 1.0, 0.0), axis=1, keepdims=True)
    eq_rank = _cumsum_lanes(jnp.where(eq, 1.0, 0.0).astype(BF16))
    sel = jnp.where(gt, 1.0, jnp.where(eq, jnp.where(eq_rank <= need, 1.0, 0.0), 0.0))
    slot = _cumsum_lanes(sel.astype(BF16)) - 1.0
    pos_ref[...] = jnp.where(sel > 0.5, slot, -1.0).astype(I32).reshape(bb, e, n)


def _topk(aff, cap):
    bsz, e, n = aff.shape
    return pl.pallas_call(
        functools.partial(_topk_body, cap=cap), grid=(1,),
        in_specs=[pl.BlockSpec((bsz, e, n), lambda i: (0, 0, 0))],
        out_specs=pl.BlockSpec((bsz, e, n), lambda i: (0, 0, 0)),
        out_shape=jax.ShapeDtypeStruct((bsz, e, n), I32),
        compiler_params=_cparams(1), name="expert_choice_topk")(aff)


def _gather_body(h_ref, pos_ref, aff_ref, xg_ref, gate_ref):
    eg, cap, _ = xg_ref.shape
    n = h_ref.shape[0]
    slot = lax.broadcasted_iota(I32, (cap, n), 0)
    for e in range(eg):
        hit = slot == pos_ref[0, e]
        onehot = jnp.where(hit, 1.0, 0.0).astype(BF16)
        xg_ref[e] = _dot(onehot, h_ref[...]).astype(BF16)
        gate_ref[e] = jnp.sum(jnp.where(hit, aff_ref[0, e], 0.0), axis=1, keepdims=True)


def _gather(h, pos4, aff4, bsz, n, cap, eg):
    d = h.shape[1]
    e = pos4.shape[1]
    return pl.pallas_call(
        _gather_body, grid=(bsz, e // eg),
        in_specs=[pl.BlockSpec((n, d), lambda b, g: (b, 0)),
                  pl.BlockSpec((1, eg, 1, n), lambda b, g: (b, g, 0, 0)),
                  pl.BlockSpec((1, eg, 1, n), lambda b, g: (b, g, 0, 0))],
        out_specs=[pl.BlockSpec((eg, cap, d), lambda b, g: (g, b, 0)),
                   pl.BlockSpec((eg, cap, 1), lambda b, g: (g, b, 0))],
        out_shape=[jax.ShapeDtypeStruct((e, bsz * cap, d), BF16),
                   jax.ShapeDtypeStruct((e, bsz * cap, 1), F32)],
        compiler_params=_cparams(2), name="moe_gather")(h, pos4, aff4)


def _ffn_body(x_ref, gate_ref, w1_ref, w3_ref, w2_ref, o_ref, acc):
    f = pl.program_id(2)
    x = x_ref[0]
    a = _dot(x, w1_ref[0].astype(BF16))
    b = _dot(x, w3_ref[0].astype(BF16))
    part = _dot((_silu(a) * b).astype(BF16), w2_ref[0].astype(BF16))

    @pl.when(f == 0)
    def _():
        acc[...] = part

    @pl.when(f > 0)
    def _():
        acc[...] += part

    @pl.when(f == pl.num_programs(2) - 1)
    def _():
        o_ref[0] = (acc[...] * gate_ref[0]).astype(BF16)


def _ffn(xg, gate, w1, w3, w2, tm, tf):
    e, m, d = xg.shape
    ff = w1.shape[2]
    return pl.pallas_call(
        _ffn_body, grid=(e, m // tm, ff // tf),
        in_specs=[pl.BlockSpec((1, tm, d), lambda x, i, f: (x, i, 0)),
                  pl.BlockSpec((1, tm, 1), lambda x, i, f: (x, i, 0)),
                  pl.BlockSpec((1, d, tf), lambda x, i, f: (x, 0, f)),
                  pl.BlockSpec((1, d, tf), lambda x, i, f: (x, 0, f)),
                  pl.BlockSpec((1, tf, d), lambda x, i, f: (x, f, 0))],
        out_specs=pl.BlockSpec((1, tm, d), lambda x, i, f: (x, i, 0)),
        out_shape=jax.ShapeDtypeStruct((e, m, d), BF16),
        scratch_shapes=[pltpu.VMEM((tm, d), F32)],
        compiler_params=_cparams(3), name="expert_ffn")(xg, gate, w1, w3, w2)


def _combine_body(pos_ref, y_ref, x_ref, m_ref, fg_ref, o_ref, acc, *, final):
    e = pl.program_id(2)
    cap = y_ref.shape[1]
    tt = x_ref.shape[0]
    onehot = jnp.where(lax.broadcasted_iota(I32, (cap, tt), 0) == pos_ref[0, 0], 1.0, 0.0).astype(BF16)
    part = _dot_tn(onehot, y_ref[0])

    @pl.when(e == 0)
    def _():
        acc[...] = part

    @pl.when(e > 0)
    def _():
        acc[...] += part

    @pl.when(e == pl.num_programs(2) - 1)
    def _():
        xn = x_ref[...] + m_ref[5:6, :] * acc[...]
        if final:
            xn = xn * lax.rsqrt(jnp.mean(xn * xn, axis=-1, keepdims=True) + EPS) * fg_ref[...]
        o_ref[...] = xn


def _combine(pos4, y, x, mod_l, row_fn, final_g, bsz, n, cap, tt, final):
    t, d = x.shape
    e = y.shape[0]
    nt = n // tt
    return pl.pallas_call(
        functools.partial(_combine_body, final=final), grid=(bsz, nt, e),
        in_specs=[pl.BlockSpec((1, 1, 1, tt), lambda b, i, x_: (b, x_, 0, i)),
                  pl.BlockSpec((1, cap, d), lambda b, i, x_: (x_, b, 0)),
                  pl.BlockSpec((tt, d), lambda b, i, x_: (b * nt + i, 0)),
                  pl.BlockSpec((None, 6, d), lambda b, i, x_: (row_fn(b), 0, 0)),
                  pl.BlockSpec((1, d), lambda b, i, x_: (0, 0))],
        out_specs=pl.BlockSpec((tt, d), lambda b, i, x_: (b * nt + i, 0)),
        out_shape=jax.ShapeDtypeStruct((t, d), F32),
        scratch_shapes=[pltpu.VMEM((tt, d), F32)],
        compiler_params=_cparams(3), name="moe_combine")(pos4, y, x, mod_l, final_g.reshape(1, d))


def _ec_moe(x, mod_l, row_fn, norm_g, rw_t, w1, w3, w2, final_g, bsz, n, final):
    e = rw_t.shape[0]
    cap = EC_FACTOR * n // e
    tt = min(n, 512)
    h, aff = _router(x, mod_l, row_fn, norm_g, rw_t, bsz, n, tt)
    pos = _topk(aff, cap)
    pos4 = pos.reshape(bsz, e, 1, n)
    eg = e if cap * e <= 1024 else 1
    xg, gate = _gather(h, pos4, aff.reshape(bsz, e, 1, n), bsz, n, cap, eg)
    m = bsz * cap
    tm = m if m <= 1024 else 1024
    y = _ffn(xg, gate, w1, w3, w2, tm, 256)
    return _combine(pos4, y, x, mod_l, row_fn, final_g, bsz, n, cap, tt, final)


def _run_group(x3, mod, row_of_request, states, p, with_rope):
    bsz, n, d = x3.shape
    x = x3.reshape(bsz * n, d)
    tm = min(n, 1024) if row_of_request is not None else min(bsz * n, 1024)
    if row_of_request is None:
        tile_row = lambda i: 0
        req_row = lambda b: 0
    else:
        tiles_per_req = n // tm
        tile_row = lambda i: row_of_request + i // tiles_per_req
        req_row = lambda b: row_of_request + b
    s_lru, s_ret, s_gdn = states
    outs = {}
    depth = p["mod"].shape[0]
    for l in range(depth):
        mod_l = mod[l]
        last = l == depth - 1
        if l % 2 == 0:
            e = l // 2
            w_in = p["w_in0"][e]
            d_a = p["conv_a"].shape[-1]
            z_a = _inproj(x, mod_l, tile_row, p["norm1_g"][l], w_in, 0, 2 * d_a, F32, tm, 512, 0, 1)
            z_b = _inproj(x, mod_l, tile_row, p["norm1_g"][l], w_in, 2 * d_a, w_in.shape[1] - 2 * d_a,
                          ACT_DTYPE, tm, 512, 0, 1)
            h0 = jnp.zeros((bsz, 2, d_a), F32) if s_lru is None else s_lru[:, e]
            out_a, st_a = _rglru(z_a, bsz, n, p["conv_a"][e], p["lru_wa"][e], p["lru_wi"][e],
                                 p["lru_ba"][e], p["lru_bi"][e], p["lru_lam"][e], h0)
            out_b, st_b = _retention(z_b, bsz, n, p["ret_decay"][e], _rope_tables(n) if with_rope else None,
                                     None if s_ret is None else s_ret[:, e])
            outs.setdefault("lru", []).append(st_a)
            outs.setdefault("ret", []).append(st_b)
            x = _outproj([out_a, out_b], p["w_out0"][e], x, mod_l, tile_row, tm, 512, 2)
        else:
            o = l // 2
            w_in = p["w_in1"][o]
            nz = H_QK * DK_C * 2 + 2 * H_V * DV_C
            z1 = _inproj(x, mod_l, tile_row, p["norm1_g"][l], w_in, 0, nz, ACT_DTYPE, tm, 512, 0, 1)
            ab = _inproj(x, mod_l, tile_row, p["norm1_g"][l], w_in, nz, 4 * H_V, F32, tm, 4 * H_V, 0, 1)
            out_c, st_c = _gdn(z1, ab, bsz, n, p["conv_c"][o], p["gdn_a_log"][o], p["gdn_dt_bias"][o],
                               p["gdn_norm_g"][o], None if s_gdn is None else s_gdn[:, o])
            outs.setdefault("gdn", []).append(st_c)
            x = _outproj([out_c], p["w_out1"][o], x, mod_l, tile_row, tm, 512, 2)
        x = _ec_moe(x, mod_l, req_row, p["norm2_g"][l], p["router_t"][l], p["exp_w1"][l], p["exp_w3"][l],
                    p["exp_w2"][l], p["final_g"], bsz, n, last)
    return x.reshape(bsz, n, d), outs


def kernel(x_prompt, x_sample, state_rglru, state_ret, state_gdn, c, c_ctx, mod_w, mod_b, norm1_g, norm2_g, w_in0, conv_a, lru_wa, lru_ba, lru_wi, lru_bi, lru_lam, ret_decay, w_out0, w_in1, conv_c, gdn_a_log, gdn_dt_bias, gdn_norm_g, w_out1, router_w, exp_w1, exp_w3, exp_w2, final_g):
    pad = (-(c.shape[0] + 1)) % 8
    cond = jnp.concatenate([c_ctx[None, :], c, jnp.zeros((pad, c.shape[1]), F32)], axis=0)
    mod = _modulation(cond, mod_w, mod_b)
    p = dict(mod=mod, norm1_g=norm1_g, norm2_g=norm2_g,
             w_in0=w_in0.astype(BF16), conv_a=conv_a, lru_wa=lru_wa, lru_ba=lru_ba, lru_wi=lru_wi,
             lru_bi=lru_bi, lru_lam=lru_lam, ret_decay=ret_decay, w_out0=w_out0.astype(BF16),
             w_in1=w_in1.astype(BF16), conv_c=conv_c, gdn_a_log=gdn_a_log, gdn_dt_bias=gdn_dt_bias,
             gdn_norm_g=gdn_norm_g, w_out1=w_out1.astype(BF16), router_t=jnp.swapaxes(router_w, 1, 2),
             exp_w1=exp_w1, exp_w3=exp_w3, exp_w2=exp_w2, final_g=final_g)
    y_prompt, st = _run_group(x_prompt, mod, None, (None, None, None), p, False)
    y_sample, _ = _run_group(x_sample, mod, 1, (state_rglru, state_ret, state_gdn), p, True)
    return (y_prompt, y_sample, jnp.stack(st["lru"], axis=1), jnp.stack(st["ret"], axis=1),
            jnp.stack(st["gdn"], axis=1))
```

```python
import functools

import jax
import jax.numpy as jnp
from jax import lax
from jax.experimental import pallas as pl
from jax.experimental.pallas import tpu as pltpu

F32 = jnp.float32
BF16 = jnp.bfloat16
I32 = jnp.int32
ACT_DTYPE = BF16
EPS = 1e-6
GRID_W = 64
LRU_C = 8.0
ROPE_BASE = 10000.0
H_A, BLK_A = 16, 128
H_B, DK_B, DV_B = 8, 128, 256
H_QK, H_V, DK_C, DV_C = 16, 32, 128, 128
GDN_C = 64
GDN_GROUP = 4
RET_C = 256
N_EXPERTS = 16
EC_FACTOR = 2
V7X_VMEM_LIMIT = 56 * 1024 * 1024


def _cparams(n_axes):
    return pltpu.CompilerParams(dimension_semantics=("arbitrary",) * n_axes,
                                vmem_limit_bytes=V7X_VMEM_LIMIT)


def _silu(x):
    return x * jax.nn.sigmoid(x)


def _softplus(x):
    return jnp.maximum(x, 0.0) + jnp.log1p(jnp.exp(-jnp.abs(x)))


def _gelu_tanh(x):
    return 0.5 * x * (1.0 + jnp.tanh(0.7978845608028654 * (x + 0.044715 * (x * x * x))))


def _adaln(x, g, shift, scale):
    r = lax.rsqrt(jnp.mean(x * x, axis=-1, keepdims=True) + EPS)
    return x * r * g * (1.0 + scale) + shift


def _dot(a, b):
    return jnp.dot(a, b, preferred_element_type=F32)


def _dot_nt(a, b):
    return lax.dot_general(a, b, (((1,), (1,)), ((), ())), preferred_element_type=F32)


def _dot_tn(a, b):
    return lax.dot_general(a, b, (((0,), (0,)), ((), ())), preferred_element_type=F32)


def _mod_body(c_ref, w_ref, b_ref, o_ref):
    s = _silu(c_ref[...]).astype(BF16)
    o_ref[0] = _dot(s, w_ref[0].astype(BF16)) + b_ref[0]


def _modulation(cond, mod_w, mod_b):
    n_l, d, d6 = mod_w.shape
    r = cond.shape[0]
    tn = 1024
    out = pl.pallas_call(
        _mod_body, grid=(n_l, d6 // tn),
        in_specs=[pl.BlockSpec((r, d), lambda l, j: (0, 0)),
                  pl.BlockSpec((1, d, tn), lambda l, j: (l, 0, j)),
                  pl.BlockSpec((1, 1, tn), lambda l, j: (l, 0, j))],
        out_specs=pl.BlockSpec((1, r, tn), lambda l, j: (l, 0, j)),
        out_shape=jax.ShapeDtypeStruct((n_l, r, d6), F32),
        compiler_params=_cparams(2), name="modulation")(cond, mod_w, mod_b.reshape(n_l, 1, d6))
    return out.reshape(n_l, r, 6, d)


def _inproj_body(x_ref, m_ref, g_ref, w_ref, o_ref, h_scr, *, shift_idx, scale_idx):
    @pl.when(pl.program_id(1) == 0)
    def _():
        h = _adaln(x_ref[...], g_ref[...], m_ref[shift_idx:shift_idx + 1, :], m_ref[scale_idx:scale_idx + 1, :])
        h_scr[...] = h.astype(BF16)
    o_ref[...] = _dot(h_scr[...], w_ref[...]).astype(o_ref.dtype)


def _inproj(x, mod_l, row_fn, g, w, col0, ncols, out_dtype, tm, tn, shift_idx, scale_idx):
    t, d = x.shape
    c0 = col0 // tn
    return pl.pallas_call(
        functools.partial(_inproj_body, shift_idx=shift_idx, scale_idx=scale_idx),
        grid=(t // tm, ncols // tn),
        in_specs=[pl.BlockSpec((tm, d), lambda i, j: (i, 0)),
                  pl.BlockSpec((None, 6, d), lambda i, j: (row_fn(i), 0, 0)),
                  pl.BlockSpec((1, d), lambda i, j: (0, 0)),
                  pl.BlockSpec((d, tn), lambda i, j: (0, c0 + j))],
        out_specs=pl.BlockSpec((tm, tn), lambda i, j: (i, j)),
        out_shape=jax.ShapeDtypeStruct((t, ncols), out_dtype),
        scratch_shapes=[pltpu.VMEM((tm, d), BF16)],
        compiler_params=_cparams(2), name="adaln_inproj")(x, mod_l, g.reshape(1, d), w)


def _outproj_body(*refs, n_in, gate_idx):
    ins = refs[:n_in]
    w_ref, x_ref, m_ref, o_ref = refs[n_in:]
    acc = None
    k0 = 0
    for a in ins:
        kk = a.shape[1]
        part = _dot(a[...], w_ref[k0:k0 + kk, :])
        acc = part if acc is None else acc + part
        k0 += kk
    o_ref[...] = x_ref[...] + m_ref[gate_idx:gate_idx + 1, :] * acc


def _outproj(ins, w, x, mod_l, row_fn, tm, tn, gate_idx):
    t, d = x.shape
    ktot = w.shape[0]
    in_specs = [pl.BlockSpec((tm, a.shape[1]), lambda i, j: (i, 0)) for a in ins]
    in_specs += [pl.BlockSpec((ktot, tn), lambda i, j: (0, j)),
                 pl.BlockSpec((tm, tn), lambda i, j: (i, j)),
                 pl.BlockSpec((None, 6, tn), lambda i, j: (row_fn(i), 0, j))]
    return pl.pallas_call(
        functools.partial(_outproj_body, n_in=len(ins), gate_idx=gate_idx),
        grid=(t // tm, d // tn), in_specs=in_specs,
        out_specs=pl.BlockSpec((tm, tn), lambda i, j: (i, j)),
        out_shape=jax.ShapeDtypeStruct((t, d), F32),
        compiler_params=_cparams(2), name="outproj_residual")(*ins, w, x, mod_l)


def _fill_padded(xpad_ref, x_ref):
    n = x_ref.shape[0]
    zeros = jnp.zeros((8, xpad_ref.shape[1]), F32)
    xpad_ref[0:8, :] = zeros
    xpad_ref[n + 8:n + 16, :] = zeros
    xpad_ref[8:n + 8, :] = x_ref[...].astype(F32)


def _conv_rows(xpad_ref, w, s, rows):
    acc = None
    for k in range(4):
        term = w[k:k + 1, :] * xpad_ref[s + 6 + k:s + 6 + k + rows, :]
        acc = term if acc is None else acc + term
    return acc


def _lru_body(gate_ref, x_ref, cw_ref, wa_ref, wi_ref, ba_ref, bi_ref, lam_ref, h0_ref,
              out_ref, st_ref, xpad, a_f, b_f, a_b, b_b, p_f, h_f, p_b, h_b, *, rows):
    n, cw = x_ref.shape
    nblk = cw // BLK_A
    _fill_padded(xpad, x_ref)
    w = cw_ref[...]
    scr = ((a_f, b_f), (a_b, b_b))
    for s in range(0, n, rows):
        xa = _conv_rows(xpad, w, s, rows)
        xab = xa.astype(BF16)
        for d in range(2):
            r_parts, i_parts = [], []
            for i in range(cw // BLK_A):
                xb = xab[:, i * BLK_A:(i + 1) * BLK_A]
                r_parts.append(_dot(xb, wa_ref[d, i].astype(BF16)))
                i_parts.append(_dot(xb, wi_ref[d, i].astype(BF16)))
            r = jax.nn.sigmoid(jnp.concatenate(r_parts, axis=1) + ba_ref[d:d + 1, :])
            ig = jax.nn.sigmoid(jnp.concatenate(i_parts, axis=1) + bi_ref[d:d + 1, :])
            log_a = (-LRU_C) * r * _softplus(-lam_ref[d:d + 1, :])
            a = jnp.exp(log_a)
            mult = jnp.sqrt(jnp.tanh(-log_a) * (a * a + 1.0))
            b = mult * (ig * xa)
            for i in range(nblk):
                scr[d][0][i, s:s + rows, :] = a[:, i * BLK_A:(i + 1) * BLK_A]
                scr[d][1][i, s:s + rows, :] = b[:, i * BLK_A:(i + 1) * BLK_A]

    seg = n // 8

    def step(i, carry):
        tb = seg - 1 - i
        rf = pl.ds(i, 8, stride=seg)
        rb = pl.ds(tb, 8, stride=seg)
        wf = pl.ds(pl.multiple_of(i * 8, 8), 8)
        wb = pl.ds(pl.multiple_of(tb * 8, 8), 8)
        out = []
        for j in range(nblk):
            pf, hf, pb, hb = carry[4 * j:4 * j + 4]
            af = a_f[j, rf, :]
            ab = a_b[j, rb, :]
            hf = af * hf + b_f[j, rf, :]
            hb = ab * hb + b_b[j, rb, :]
            pf = af * pf
            pb = ab * pb
            p_f[j, wf, :] = pf
            h_f[j, wf, :] = hf
            p_b[j, wb, :] = pb
            h_b[j, wb, :] = hb
            out += [pf, hf, pb, hb]
        return tuple(out)

    ones = jnp.ones((8, BLK_A), F32)
    zeros = jnp.zeros((8, BLK_A), F32)
    lax.fori_loop(0, seg, step, (ones, zeros, ones, zeros) * nblk, unroll=4)

    carry_f = [h0_ref[0, 0:1, j * BLK_A:(j + 1) * BLK_A] for j in range(nblk)]
    for s in range(8):
        sl = slice(s * seg, (s + 1) * seg)
        il = pl.ds(s, seg, stride=8)
        for j in range(nblk):
            h = h_f[j, il, :] + p_f[j, il, :] * carry_f[j]
            b_f[j, sl, :] = h
            carry_f[j] = h[seg - 1:seg, :]
    st_ref[0, 0:1, :] = jnp.concatenate(carry_f, axis=1)
    carry_b = [h0_ref[0, 1:2, j * BLK_A:(j + 1) * BLK_A] for j in range(nblk)]
    for s in range(7, -1, -1):
        sl = slice(s * seg, (s + 1) * seg)
        il = pl.ds(s, seg, stride=8)
        hsum = []
        for j in range(nblk):
            h = h_b[j, il, :] + p_b[j, il, :] * carry_b[j]
            carry_b[j] = h[0:1, :]
            hsum.append(b_f[j, sl, :] + h)
        out_ref[sl, :] = (_gelu_tanh(gate_ref[sl, :]) * jnp.concatenate(hsum, axis=1)).astype(BF16)
    st_ref[0, 1:2, :] = jnp.concatenate(carry_b, axis=1)


def _rglru(z_a, bsz, n, conv_w, wa, wi, ba, bi, lam, h0):
    d_a = conv_w.shape[1]
    cw = 256
    nb = d_a // cw
    hb = cw // BLK_A
    rows = min(n, 256)
    return pl.pallas_call(
        functools.partial(_lru_body, rows=rows), grid=(bsz, nb),
        in_specs=[pl.BlockSpec((n, cw), lambda b, j: (b, j)),
                  pl.BlockSpec((n, cw), lambda b, j: (b, nb + j)),
                  pl.BlockSpec((4, cw), lambda b, j: (0, j)),
                  pl.BlockSpec((2, hb, BLK_A, BLK_A), lambda b, j: (0, j, 0, 0)),
                  pl.BlockSpec((2, hb, BLK_A, BLK_A), lambda b, j: (0, j, 0, 0)),
                  pl.BlockSpec((2, cw), lambda b, j: (0, j)),
                  pl.BlockSpec((2, cw), lambda b, j: (0, j)),
                  pl.BlockSpec((2, cw), lambda b, j: (0, j)),
                  pl.BlockSpec((1, 2, cw), lambda b, j: (b, 0, j))],
        out_specs=[pl.BlockSpec((n, cw), lambda b, j: (b, j)),
                   pl.BlockSpec((1, 2, cw), lambda b, j: (b, 0, j))],
        out_shape=[jax.ShapeDtypeStruct((bsz * n, d_a), BF16),
                   jax.ShapeDtypeStruct((bsz, 2, d_a), F32)],
        scratch_shapes=[pltpu.VMEM((n + 16, cw), F32)] + [pltpu.VMEM((hb, n, BLK_A), F32)] * 8,
        compiler_params=_cparams(2), name="rglru")(z_a, z_a, conv_w, wa, wi, ba, bi, lam, h0)


def _ret_body(*refs, use_rope, has_state, chunk):
    q_ref, k_ref, v_ref, g_ref, dec_ref = refs[:5]
    pos = 5
    if use_rope:
        cos_ref, sin_ref = refs[pos:pos + 2]
        pos += 2
    if has_state:
        s0_ref = refs[pos]
        pos += 1
    o_ref, st_ref, rb_scr = refs[pos:pos + 3]
    n = q_ref.shape[0]
    c_len = chunk
    nc = n // c_len
    head = pl.program_id(1)

    lg = -_softplus(-dec_ref[...])
    lane = lax.broadcasted_iota(I32, lg.shape, 1)
    lgh = jnp.sum(jnp.where(lane == head, lg, 0.0), axis=1, keepdims=True)
    lgf, lgb = lgh[0:1, :], lgh[1:2, :]

    def prep(ref, c, sc):
        x = ref[c * c_len:(c + 1) * c_len, :].astype(F32) * sc
        if use_rope:
            x = (x * cos_ref[c * c_len:(c + 1) * c_len, :]
                 + pltpu.roll(x, DK_B // 2, 1) * sin_ref[c * c_len:(c + 1) * c_len, :])
        return x

    ii = lax.broadcasted_iota(I32, (c_len, 1), 0).astype(F32)
    jj = lax.broadcasted_iota(I32, (1, c_len), 1).astype(F32)
    rel = ii - jj
    dmask = jnp.where(rel > 0.0, jnp.exp(lgf * jnp.maximum(rel, 0.0)),
                      jnp.where(rel < 0.0, jnp.exp(lgb * jnp.maximum(-rel, 0.0)), 2.0))
    qf_dec = jnp.exp(lgf * (ii + 1.0))
    qb_dec = jnp.exp(lgb * (c_len - ii))
    kf_dec = jnp.exp(lgf * (c_len - 1.0 - ii))
    kb_dec = jnp.exp(lgb * ii)
    gf_c = jnp.exp(lgf * c_len)
    gb_c = jnp.exp(lgb * c_len)

    r_b = s0_ref[0, 1, 0] if has_state else jnp.zeros((DK_B, DV_B), F32)
    for c in range(nc - 1, -1, -1):
        rb_scr[c] = r_b
        k = prep(k_ref, c, 1.0)
        v = v_ref[c * c_len:(c + 1) * c_len, :].astype(BF16)
        r_b = r_b * gb_c + _dot_tn((k * kb_dec).astype(BF16), v)
    st_ref[0, 1, 0] = r_b

    r_f = s0_ref[0, 0, 0] if has_state else jnp.zeros((DK_B, DV_B), F32)
    for c in range(nc):
        q = prep(q_ref, c, DK_B ** -0.5)
        k = prep(k_ref, c, 1.0)
        v = v_ref[c * c_len:(c + 1) * c_len, :].astype(BF16)
        s = _dot_nt(q.astype(BF16), k.astype(BF16)) * dmask
        o = _dot(s.astype(BF16), v)
        qd = jnp.concatenate([q * qf_dec, q * qb_dec], axis=1).astype(BF16)
        rcat = jnp.concatenate([r_f, rb_scr[c]], axis=0).astype(BF16)
        o = o + _dot(qd, rcat)
        r_f = r_f * gf_c + _dot_tn((k * kf_dec).astype(BF16), v)
        on = o * lax.rsqrt(jnp.mean(o * o, axis=-1, keepdims=True) + EPS)
        g = g_ref[c * c_len:(c + 1) * c_len, :].astype(F32)
        o_ref[c * c_len:(c + 1) * c_len, :] = (_silu(g) * on).astype(BF16)
    st_ref[0, 0, 0] = r_f


def _retention(z_b, bsz, n, dec, rope, s0):
    nq = H_B
    chunk = min(n, RET_C)
    in_specs = [pl.BlockSpec((n, DK_B), lambda b, h: (b, h)),
                pl.BlockSpec((n, DK_B), lambda b, h: (b, nq + h)),
                pl.BlockSpec((n, DV_B), lambda b, h: (b, nq + h)),
                pl.BlockSpec((n, DV_B), lambda b, h: (b, 2 * nq + h)),
                pl.BlockSpec((2, H_B), lambda b, h: (0, 0))]
    args = [z_b, z_b, z_b, z_b, dec]
    if rope is not None:
        in_specs += [pl.BlockSpec((n, DK_B), lambda b, h: (0, 0))] * 2
        args += list(rope)
    if s0 is not None:
        in_specs.append(pl.BlockSpec((1, 2, 1, DK_B, DV_B), lambda b, h: (b, 0, h, 0, 0)))
        args.append(s0)
    return pl.pallas_call(
        functools.partial(_ret_body, use_rope=rope is not None, has_state=s0 is not None, chunk=chunk),
        grid=(bsz, H_B), in_specs=in_specs,
        out_specs=[pl.BlockSpec((n, DV_B), lambda b, h: (b, h)),
                   pl.BlockSpec((1, 2, 1, DK_B, DV_B), lambda b, h: (b, 0, h, 0, 0))],
        out_shape=[jax.ShapeDtypeStruct((bsz * n, H_B * DV_B), BF16),
                   jax.ShapeDtypeStruct((bsz, 2, H_B, DK_B, DV_B), F32)],
        scratch_shapes=[pltpu.VMEM((n // chunk, DK_B, DV_B), F32)],
        compiler_params=_cparams(2), name="retention")(*args)


def _rope_tables(n):
    rows = n // GRID_W
    n_freq = DK_B // 4
    row = jnp.repeat(jnp.arange(rows, dtype=F32), GRID_W)
    col = (jnp.arange(n) % GRID_W).astype(F32)
    inv = ROPE_BASE ** (-jnp.arange(n_freq, dtype=F32) / n_freq)
    ang = jnp.concatenate([row[:, None] * inv, col[:, None] * inv], axis=-1)
    cos, sin = jnp.cos(ang), jnp.sin(ang)
    return jnp.concatenate([cos, cos], axis=-1), jnp.concatenate([-sin, sin], axis=-1)


def _gdn_body(*refs, has_state, rows):
    (zq_ref, zk_ref, zv_ref, zg_ref, cq_ref, ck_ref, cv_ref, ab_ref, gr_ref,
     a8_ref, dt8_ref, ar_ref, dtr_ref, ng_ref) = refs[:14]
    pos = 14
    if has_state:
        s0_ref = refs[pos]
        pos += 1
    (o_ref, st_ref, xpad, xpad_v, q_scr, k_scr, v_scr, o_scr, bg_scr, grow_scr, s_scr,
     m_scr, u_scr, w_scr, at_scr, qd_scr, kd_scr, eg_scr) = refs[pos:]
    n = zq_ref.shape[0]
    c_len = GDN_C
    c4 = 4 * c_len
    nc = n // c_len
    lg = c_len.bit_length() - 1

    _fill_padded(xpad, zq_ref)
    for s in range(0, n, rows):
        q = _silu(_conv_rows(xpad, cq_ref[...], s, rows))
        q = q * lax.rsqrt(jnp.sum(q * q, axis=-1, keepdims=True) + EPS) * (DK_C ** -0.5)
        q_scr[s:s + rows, :] = q.astype(BF16)
    _fill_padded(xpad, zk_ref)
    for s in range(0, n, rows):
        k = _silu(_conv_rows(xpad, ck_ref[...], s, rows))
        k = k * lax.rsqrt(jnp.sum(k * k, axis=-1, keepdims=True) + EPS)
        k_scr[s:s + rows, :] = k.astype(BF16)
    _fill_padded(xpad_v, zv_ref)
    for s in range(0, n, rows):
        v_scr[s:s + rows, :] = _silu(_conv_rows(xpad_v, cv_ref[...], s, rows))

    ab = ab_ref[0, 0]
    gcols = -jnp.exp(a8_ref[0]) * _softplus(ab + dt8_ref[0])
    w_is_beta = ((lax.broadcasted_iota(I32, ab.shape, 1) >> 1) & 1) == 0
    bg_scr[...] = jnp.where(w_is_beta, jax.nn.sigmoid(ab), gcols)
    grow_scr[...] = -jnp.exp(ar_ref[0]) * _softplus(gr_ref[0, 0] + dtr_ref[0])
    o_scr[...] = jnp.zeros(o_scr.shape, F32)
    for d in range(2):
        if has_state:
            s_scr[d] = jnp.concatenate([s0_ref[0, d, 0], s0_ref[0, d, 1]], axis=0)
        else:
            s_scr[d] = jnp.zeros((2 * DK_C, DV_C), F32)

    ii = lax.broadcasted_iota(I32, (c4, c4), 0)
    jj = lax.broadcasted_iota(I32, (c4, c4), 1)
    blk = (ii >> lg) == (jj >> lg)
    fwd = (ii >> (lg + 1)) == 0
    one = lambda m: jnp.where(m, 1.0, 0.0)
    m_scr[0] = jnp.where(blk, jnp.where(fwd, one(jj <= ii), one(jj >= ii)), 0.0)
    m_scr[1] = jnp.where(blk, jnp.where(fwd, one(ii <= jj), one(ii >= jj)), 0.0)
    m_scr[2] = jnp.where(blk, jnp.where(fwd, one((jj & (c_len - 1)) == c_len - 1),
                                        one((jj & (c_len - 1)) == 0)), 0.0)
    m_scr[3] = one((ii >> 3) == (jj >> 3))
    n_merge = lg - 3
    for t in range(n_merge):
        sh = 3 + t
        m_scr[4 + t] = jnp.where((ii >> (sh + 1)) == (jj >> (sh + 1)), one((ii >> sh) != (jj >> sh)), 0.0)
    eye = one(ii == jj)
    row = lax.broadcasted_iota(I32, (c4, 1), 0)
    head0 = one(((row >> lg) & 1) == 0)
    head1 = 1.0 - head0

    def tri_inverse(a_mats):
        a_d = [a * m_scr[3] for a in a_mats]
        adb = [a.astype(BF16) for a in a_d]
        a2b = [_dot(a, a).astype(BF16) for a in adb]
        ps = [eye - a for a in a_d]
        both = [_dot(jnp.concatenate([p.astype(BF16), a2], axis=0), a2) for p, a2 in zip(ps, a2b)]
        ps = [p + b[:c4] for p, b in zip(ps, both)]
        ps = [p + _dot(p.astype(BF16), b[c4:].astype(BF16)) for p, b in zip(ps, both)]
        for t in range(n_merge):
            pbs = [p.astype(BF16) for p in ps]
            ys = [_dot((a * m_scr[4 + t]).astype(BF16), pb).astype(BF16) for a, pb in zip(a_mats, pbs)]
            ps = [p - _dot(pb, y) for p, pb, y in zip(ps, pbs, ys)]
        return ps

    def prep_group(gi, carry):
        cs = [gi * GDN_GROUP + t for t in range(GDN_GROUP)]
        rows_c = [pl.ds(pl.multiple_of(c * c_len, c_len), c_len) for c in cs]
        out_rows = [pl.ds(pl.multiple_of(c * c4, c4), c4) for c in cs]
        kc4 = [jnp.concatenate([k_scr[r, :]] * 4, axis=0) for r in rows_c]
        qc4 = [jnp.concatenate([q_scr[r, :]] * 4, axis=0) for r in rows_c]
        kk = [_dot_nt(k, k) for k in kc4]
        qk = [_dot_nt(q, k) for q, k in zip(qc4, kc4)]
        incl = m_scr[0]
        beta, gc_col, g_last, decay, a_mats = [], [], [], [], []
        for t in range(GDN_GROUP):
            bg = bg_scr[rows_c[t], :]
            beta.append(jnp.concatenate([bg[:, 0:1], bg[:, 1:2], bg[:, 4:5], bg[:, 5:6]], axis=0))
            gcol = jnp.concatenate([bg[:, 2:3], bg[:, 3:4], bg[:, 6:7], bg[:, 7:8]], axis=0)
            grow = grow_scr[pl.ds(cs[t], 1), :]
            gc_col.append(jnp.sum(incl * grow, axis=1, keepdims=True))
            gc_row = jnp.sum(m_scr[1] * gcol, axis=0, keepdims=True)
            decay.append(jnp.exp((gc_col[t] - gc_row) * incl) * incl)
            g_last.append(jnp.sum(m_scr[2] * gc_row, axis=1, keepdims=True))
            a_mats.append((beta[t] * kk[t] * decay[t]) * (incl - eye))
        t_inv = tri_inverse(a_mats)
        for t in range(GDN_GROUP):
            kf = kc4[t].astype(F32)
            qf = qc4[t].astype(F32)
            egc = jnp.exp(gc_col[t])
            vc = v_scr[rows_c[t], :]
            v4 = jnp.concatenate([vc[:, :DV_C], vc[:, DV_C:], vc[:, :DV_C], vc[:, DV_C:]], axis=0)
            kb = kf * (beta[t] * egc)
            rhs = jnp.concatenate([v4 * beta[t], kb * head0, kb * head1], axis=1).astype(BF16)
            uw = _dot(t_inv[t].astype(BF16), rhs)
            u_scr[out_rows[t], :] = uw[:, :DV_C]
            w_scr[out_rows[t], :] = uw[:, DV_C:].astype(BF16)
            attn = qk[t] * decay[t]
            at_scr[out_rows[t], :] = jnp.concatenate(
                [attn[:2 * c_len, :2 * c_len], attn[2 * c_len:, 2 * c_len:]], axis=0).astype(BF16)
            qe = qf * egc
            qd_scr[out_rows[t], :] = jnp.concatenate([qe * head0, qe * head1], axis=1).astype(BF16)
            ke = kf * jnp.exp(g_last[t] - gc_col[t])
            kd_scr[out_rows[t], :] = jnp.concatenate([ke * head0, ke * head1], axis=1).astype(BF16)
            egl = jnp.exp(g_last[t])
            for s in range(4):
                eg_scr[pl.ds(cs[t] * 4 + s, 1), :] = jnp.broadcast_to(egl[s * c_len:s * c_len + 1, :], (1, DV_C))
        return carry

    lax.fori_loop(0, nc // GDN_GROUP, prep_group, 0)

    def scan_chunk(c, carry):
        ccs = [c, nc - 1 - c]
        rows_d = [pl.ds(pl.multiple_of(ccs[d] * c4 + d * 2 * c_len, 2 * c_len), 2 * c_len) for d in range(2)]
        s_cat = [s_scr[d] for d in range(2)]
        sb = [s.astype(BF16) for s in s_cat]
        ws = [_dot(w_scr[rows_d[d], :], sb[d]) for d in range(2)]
        qs = [_dot(qd_scr[rows_d[d], :], sb[d]) for d in range(2)]
        vnb = [(u_scr[rows_d[d], :] - ws[d]).astype(BF16) for d in range(2)]
        kv = [_dot_tn(kd_scr[rows_d[d], :], vnb[d]) for d in range(2)]
        av = [_dot(at_scr[rows_d[d], :], vnb[d]) for d in range(2)]
        for d in range(2):
            e0 = jnp.broadcast_to(eg_scr[pl.ds(ccs[d] * 4 + 2 * d, 1), :], (DK_C, DV_C))
            e1 = jnp.broadcast_to(eg_scr[pl.ds(ccs[d] * 4 + 2 * d + 1, 1), :], (DK_C, DV_C))
            s_scr[d] = s_cat[d] * jnp.concatenate([e0, e1], axis=0) + kv[d]
            o = qs[d] + av[d]
            tok = pl.ds(pl.multiple_of(ccs[d] * c_len, c_len), c_len)
            o_scr[tok, 0:DV_C] += o[:c_len]
            o_scr[tok, DV_C:2 * DV_C] += o[c_len:]
        return carry

    lax.fori_loop(0, nc, scan_chunk, 0)

    for d in range(2):
        for r in range(2):
            st_ref[0, d, r] = s_scr[d, r * DK_C:(r + 1) * DK_C, :]
    ng = ng_ref[...]
    for s in range(0, n, rows):
        zg = zg_ref[s:s + rows, :].astype(F32)
        for r in range(2):
            o = o_scr[s:s + rows, r * DV_C:(r + 1) * DV_C]
            on = o * lax.rsqrt(jnp.mean(o * o, axis=-1, keepdims=True) + EPS) * ng
            o_ref[s:s + rows, r * DV_C:(r + 1) * DV_C] = (
                on * _silu(zg[:, r * DV_C:(r + 1) * DV_C])).astype(BF16)


def _gdn(z1, ab, bsz, n, conv_c, a_log, dt_bias, norm_g, s0):
    nc = n // GDN_C
    rows = min(n, 256)
    nqb = H_QK
    ab6 = ab.reshape(bsz, n, 2, 2, H_QK, 2)
    ab8 = ab6.transpose(0, 4, 1, 2, 3, 5).reshape(bsz, H_QK, n, 8)
    g_rows = ab6[:, :, :, 1].reshape(bsz, nc, GDN_C, 2, H_QK, 2).transpose(0, 4, 1, 3, 5, 2)
    g_rows = g_rows.reshape(bsz, H_QK, nc, 4 * GDN_C)

    def per_pair(p):
        p3 = p.astype(F32).reshape(2, H_QK, 2).transpose(1, 0, 2)
        return jnp.broadcast_to(p3[:, :, None, :], (H_QK, 2, 2, 2)).reshape(H_QK, 8)

    def per_stream(p):
        p3 = p.astype(F32).reshape(2, H_QK, 2).transpose(1, 0, 2).reshape(H_QK, 4)
        return jnp.repeat(p3, GDN_C, axis=1).reshape(H_QK, 1, 4 * GDN_C)

    a8, dt8 = per_pair(a_log), per_pair(dt_bias)
    in_specs = [pl.BlockSpec((n, DK_C), lambda b, j: (b, j)),
                pl.BlockSpec((n, DK_C), lambda b, j: (b, nqb + j)),
                pl.BlockSpec((n, 2 * DV_C), lambda b, j: (b, nqb + j)),
                pl.BlockSpec((n, 2 * DV_C), lambda b, j: (b, 2 * nqb + j)),
                pl.BlockSpec((4, DK_C), lambda b, j: (0, j)),
                pl.BlockSpec((4, DK_C), lambda b, j: (0, nqb + j)),
                pl.BlockSpec((4, 2 * DV_C), lambda b, j: (0, nqb + j)),
                pl.BlockSpec((1, 1, n, 8), lambda b, j: (b, j, 0, 0)),
                pl.BlockSpec((1, 1, nc, 4 * GDN_C), lambda b, j: (b, j, 0, 0)),
                pl.BlockSpec((1, 1, 8), lambda b, j: (j, 0, 0)),
                pl.BlockSpec((1, 1, 8), lambda b, j: (j, 0, 0)),
                pl.BlockSpec((1, 1, 4 * GDN_C), lambda b, j: (j, 0, 0)),
                pl.BlockSpec((1, 1, 4 * GDN_C), lambda b, j: (j, 0, 0)),
                pl.BlockSpec((1, DV_C), lambda b, j: (0, 0))]
    args = [z1, z1, z1, z1, conv_c, conv_c, conv_c, ab8, g_rows,
            a8.reshape(H_QK, 1, 8), dt8.reshape(H_QK, 1, 8),
            per_stream(a_log), per_stream(dt_bias), norm_g.reshape(1, DV_C)]
    if s0 is not None:
        in_specs.append(pl.BlockSpec((1, 2, 2, DK_C, DV_C), lambda b, j: (b, 0, j, 0, 0)))
        args.append(s0)
    return pl.pallas_call(
        functools.partial(_gdn_body, has_state=s0 is not None, rows=rows),
        grid=(bsz, H_QK), in_specs=in_specs,
        out_specs=[pl.BlockSpec((n, 2 * DV_C), lambda b, j: (b, j)),
                   pl.BlockSpec((1, 2, 2, DK_C, DV_C), lambda b, j: (b, 0, j, 0, 0))],
        out_shape=[jax.ShapeDtypeStruct((bsz * n, H_V * DV_C), BF16),
                   jax.ShapeDtypeStruct((bsz, 2, H_V, DK_C, DV_C), F32)],
        scratch_shapes=[pltpu.VMEM((n + 16, DK_C), F32), pltpu.VMEM((n + 16, 2 * DV_C), F32),
                        pltpu.VMEM((n, DK_C), BF16), pltpu.VMEM((n, DK_C), BF16),
                        pltpu.VMEM((n, 2 * DV_C), F32), pltpu.VMEM((n, 2 * DV_C), F32),
                        pltpu.VMEM((n, 8), F32), pltpu.VMEM((nc, 4 * GDN_C), F32),
                        pltpu.VMEM((2, 2 * DK_C, DV_C), F32),
                        pltpu.VMEM((GDN_C.bit_length(), 4 * GDN_C, 4 * GDN_C), F32),
                        pltpu.VMEM((4 * n, DV_C), F32), pltpu.VMEM((4 * n, 2 * DK_C), BF16),
                        pltpu.VMEM((4 * n, 2 * GDN_C), BF16), pltpu.VMEM((4 * n, 2 * DK_C), BF16),
                        pltpu.VMEM((4 * n, 2 * DK_C), BF16), pltpu.VMEM((4 * nc, DV_C), F32)],
        compiler_params=_cparams(2), name="gated_deltanet")(*args)


def _router_body(x_ref, m_ref, g_ref, rwt_ref, h_ref, aff_ref):
    h = _adaln(x_ref[...], g_ref[...], m_ref[3:4, :], m_ref[4:5, :])
    h_ref[...] = h.astype(BF16)
    logits = _dot_nt(rwt_ref[...].astype(BF16), h.astype(BF16))
    ex = jnp.exp(logits - jnp.max(logits, axis=0, keepdims=True))
    aff_ref[0] = ex / jnp.sum(ex, axis=0, keepdims=True)


def _router(x, mod_l, row_fn, g, rw_t, bsz, n, tt):
    t, d = x.shape
    e = rw_t.shape[0]
    nt = n // tt
    return pl.pallas_call(
        _router_body, grid=(bsz, nt),
        in_specs=[pl.BlockSpec((tt, d), lambda b, i: (b * nt + i, 0)),
                  pl.BlockSpec((None, 6, d), lambda b, i: (row_fn(b), 0, 0)),
                  pl.BlockSpec((1, d), lambda b, i: (0, 0)),
                  pl.BlockSpec((e, d), lambda b, i: (0, 0))],
        out_specs=[pl.BlockSpec((tt, d), lambda b, i: (b * nt + i, 0)),
                   pl.BlockSpec((1, e, tt), lambda b, i: (b, 0, i))],
        out_shape=[jax.ShapeDtypeStruct((t, d), BF16), jax.ShapeDtypeStruct((bsz, e, n), F32)],
        compiler_params=_cparams(2), name="adaln_router")(x, mod_l, g.reshape(1, d), rw_t)


def _cumsum_lanes(m):
    r, n = m.shape
    blk = min(n, 256)
    tri = jnp.where(lax.broadcasted_iota(I32, (blk, blk), 0) <= lax.broadcasted_iota(I32, (blk, blk), 1),
                    1.0, 0.0).astype(BF16)
    outs = []
    run = jnp.zeros((r, 1), F32)
    for c in range(n // blk):
        loc = _dot(m[:, c * blk:(c + 1) * blk], tri) + run
        outs.append(loc)
        run = loc[:, blk - 1:blk]
    return outs[0] if len(outs) == 1 else jnp.concatenate(outs, axis=1)


def _topk_body(aff_ref, pos_ref, *, cap):
    bb, e, n = aff_ref.shape
    rows = bb * e
    bits = pltpu.bitcast(aff_ref[...].reshape(rows, n), I32)
    capf = float(cap)
    prefix = jnp.zeros((rows, 1), I32)
    for bit in range(30, -1, -1):
        cand = prefix | (1 << bit)
        cnt = jnp.sum(jnp.where(bits >= cand, 1.0, 0.0), axis=1, keepdims=True)
        prefix = jnp.where(cnt >= capf, cand, prefix)
    gt = bits > prefix
    eq = bits == prefix
    need = capf - jnp.sum(jnp.where(gt, 1.0, 0.0), axis=1, keepdims=True)
    eq_rank = _cumsum_lanes(jnp.where(eq, 1.0, 0.0).astype(BF16))
    sel = jnp.where(gt, 1.0, jnp.where(eq, jnp.where(eq_rank <= need, 1.0, 0.0), 0.0))
    slot = _cumsum_lanes(sel.astype(BF16)) - 1.0
    pos_ref[...] = jnp.where(sel > 0.5, slot, -1.0).astype(I32).reshape(bb, e, n)


def _topk(aff, cap):
    bsz, e, n = aff.shape
    return pl.pallas_call(
        functools.partial(_topk_body, cap=cap), grid=(1,),
        in_specs=[pl.BlockSpec((bsz, e, n), lambda i: (0, 0, 0))],
        out_specs=pl.BlockSpec((bsz, e, n), lambda i: (0, 0, 0)),
        out_shape=jax.ShapeDtypeStruct((bsz, e, n), I32),
        compiler_params=_cparams(1), name="expert_choice_topk")(aff)


def _gather_body(h_ref, pos_ref, aff_ref, xg_ref, gate_ref):
    eg, cap, _ = xg_ref.shape
    n = h_ref.shape[0]
    slot = lax.broadcasted_iota(I32, (cap, n), 0)
    for e in range(eg):
        hit = slot == pos_ref[0, e]
        onehot = jnp.where(hit, 1.0, 0.0).astype(BF16)
        xg_ref[e] = _dot(onehot, h_ref[...]).astype(BF16)
        gate_ref[e] = jnp.sum(jnp.where(hit, aff_ref[0, e], 0.0), axis=1, keepdims=True)


def _gather(h, pos4, aff4, bsz, n, cap, eg):
    d = h.shape[1]
    e = pos4.shape[1]
    return pl.pallas_call(
        _gather_body, grid=(bsz, e // eg),
        in_specs=[pl.BlockSpec((n, d), lambda b, g: (b, 0)),
                  pl.BlockSpec((1, eg, 1, n), lambda b, g: (b, g, 0, 0)),
                  pl.BlockSpec((1, eg, 1, n), lambda b, g: (b, g, 0, 0))],
        out_specs=[pl.BlockSpec((eg, cap, d), lambda b, g: (g, b, 0)),
                   pl.BlockSpec((eg, cap, 1), lambda b, g: (g, b, 0))],
        out_shape=[jax.ShapeDtypeStruct((e, bsz * cap, d), BF16),
                   jax.ShapeDtypeStruct((e, bsz * cap, 1), F32)],
        compiler_params=_cparams(2), name="moe_gather")(h, pos4, aff4)


def _ffn_body(x_ref, gate_ref, w1_ref, w3_ref, w2_ref, o_ref, acc):
    f = pl.program_id(2)
    x = x_ref[0]
    a = _dot(x, w1_ref[0].astype(BF16))
    b = _dot(x, w3_ref[0].astype(BF16))
    part = _dot((_silu(a) * b).astype(BF16), w2_ref[0].astype(BF16))

    @pl.when(f == 0)
    def _():
        acc[...] = part

    @pl.when(f > 0)
    def _():
        acc[...] += part

    @pl.when(f == pl.num_programs(2) - 1)
    def _():
        o_ref[0] = (acc[...] * gate_ref[0]).astype(BF16)


def _ffn(xg, gate, w1, w3, w2, layer, tm, tf):
    e, m, d = xg.shape
    ff = w1.shape[3]
    return pl.pallas_call(
        _ffn_body, grid=(e, m // tm, ff // tf),
        in_specs=[pl.BlockSpec((1, tm, d), lambda x, i, f: (x, i, 0)),
                  pl.BlockSpec((1, tm, 1), lambda x, i, f: (x, i, 0)),
                  pl.BlockSpec((None, 1, d, tf), lambda x, i, f: (layer, x, 0, f)),
                  pl.BlockSpec((None, 1, d, tf), lambda x, i, f: (layer, x, 0, f)),
                  pl.BlockSpec((None, 1, tf, d), lambda x, i, f: (layer, x, f, 0))],
        out_specs=pl.BlockSpec((1, tm, d), lambda x, i, f: (x, i, 0)),
        out_shape=jax.ShapeDtypeStruct((e, m, d), BF16),
        scratch_shapes=[pltpu.VMEM((tm, d), F32)],
        compiler_params=_cparams(3), name="expert_ffn")(xg, gate, w1, w3, w2)


def _combine_body(pos_ref, y_ref, x_ref, m_ref, fg_ref, o_ref, acc, *, final):
    e = pl.program_id(2)
    eg, cap, d = y_ref.shape
    tt = x_ref.shape[0]
    rows8 = [pos_ref[0, g] for g in range(eg)] + [jnp.zeros((1, tt), I32)] * ((-eg) % 8)
    pos_rows = jnp.concatenate(rows8, axis=0).astype(F32).astype(BF16)
    eye = jnp.where(lax.broadcasted_iota(I32, (tt, tt), 0) == lax.broadcasted_iota(I32, (tt, tt), 1),
                    1.0, 0.0).astype(BF16)
    pos_cols = _dot_nt(eye, pos_rows)
    slot = lax.broadcasted_iota(I32, (tt, cap), 1).astype(F32)
    onehot = jnp.concatenate([jnp.where(slot == pos_cols[:, g:g + 1], 1.0, 0.0).astype(BF16)
                              for g in range(eg)], axis=1)
    part = _dot(onehot, y_ref[...].reshape(eg * cap, d))

    @pl.when(e == 0)
    def _():
        acc[...] = part

    @pl.when(e > 0)
    def _():
        acc[...] += part

    @pl.when(e == pl.num_programs(2) - 1)
    def _():
        xn = x_ref[...] + m_ref[5:6, :] * acc[...]
        if final:
            xn = xn * lax.rsqrt(jnp.mean(xn * xn, axis=-1, keepdims=True) + EPS) * fg_ref[...]
        o_ref[...] = xn


def _combine(pos4, y, x, mod_l, row_fn, final_g, bsz, n, cap, tt, eg, final):
    t, d = x.shape
    e = y.shape[0]
    nt = n // tt
    return pl.pallas_call(
        functools.partial(_combine_body, final=final), grid=(bsz, nt, e // eg),
        in_specs=[pl.BlockSpec((1, eg, 1, tt), lambda b, i, x_: (b, x_, 0, i)),
                  pl.BlockSpec((eg, cap, d), lambda b, i, x_: (x_, b, 0)),
                  pl.BlockSpec((tt, d), lambda b, i, x_: (b * nt + i, 0)),
                  pl.BlockSpec((None, 6, d), lambda b, i, x_: (row_fn(b), 0, 0)),
                  pl.BlockSpec((1, d), lambda b, i, x_: (0, 0))],
        out_specs=pl.BlockSpec((tt, d), lambda b, i, x_: (b * nt + i, 0)),
        out_shape=jax.ShapeDtypeStruct((t, d), F32),
        scratch_shapes=[pltpu.VMEM((tt, d), F32)],
        compiler_params=_cparams(3), name="moe_combine")(pos4, y, x, mod_l, final_g.reshape(1, d))


def _ec_moe(x, mod_l, row_fn, norm_g, rw_t, w1, w3, w2, layer, final_g, bsz, n, final):
    e = rw_t.shape[0]
    cap = EC_FACTOR * n // e
    tt = min(n, 512)
    h, aff = _router(x, mod_l, row_fn, norm_g, rw_t, bsz, n, tt)
    pos = _topk(aff, cap)
    pos4 = pos.reshape(bsz, e, 1, n)
    eg = e if cap * e <= 1024 else 1
    xg, gate = _gather(h, pos4, aff.reshape(bsz, e, 1, n), bsz, n, cap, eg)
    m = bsz * cap
    tm = m if m <= 1024 else 1024
    y = _ffn(xg, gate, w1, w3, w2, layer, tm, 256)
    eg_c = max(1, min(e, 512 // cap))
    return _combine(pos4, y, x, mod_l, row_fn, final_g, bsz, n, cap, tt, eg_c, final)


def _run_group(x3, mod, row_of_request, states, p, with_rope):
    bsz, n, d = x3.shape
    x = x3.reshape(bsz * n, d)
    tm = min(n, 1024) if row_of_request is not None else min(bsz * n, 1024)
    if row_of_request is None:
        tile_row = lambda i: 0
        req_row = lambda b: 0
    else:
        tiles_per_req = n // tm
        tile_row = lambda i: row_of_request + i // tiles_per_req
        req_row = lambda b: row_of_request + b
    s_lru, s_ret, s_gdn = states
    outs = {}
    depth = p["mod"].shape[0]
    for l in range(depth):
        mod_l = mod[l]
        last = l == depth - 1
        if l % 2 == 0:
            e = l // 2
            w_in = p["w_in0"][e]
            d_a = p["conv_a"].shape[-1]
            z_a = _inproj(x, mod_l, tile_row, p["norm1_g"][l], w_in, 0, 2 * d_a, F32, tm, 512, 0, 1)
            z_b = _inproj(x, mod_l, tile_row, p["norm1_g"][l], w_in, 2 * d_a, w_in.shape[1] - 2 * d_a,
                          ACT_DTYPE, tm, 512, 0, 1)
            h0 = jnp.zeros((bsz, 2, d_a), F32) if s_lru is None else s_lru[:, e]
            out_a, st_a = _rglru(z_a, bsz, n, p["conv_a"][e], p["lru_wa"][e], p["lru_wi"][e],
                                 p["lru_ba"][e], p["lru_bi"][e], p["lru_lam"][e], h0)
            out_b, st_b = _retention(z_b, bsz, n, p["ret_decay"][e], _rope_tables(n) if with_rope else None,
                                     None if s_ret is None else s_ret[:, e])
            outs.setdefault("lru", []).append(st_a)
            outs.setdefault("ret", []).append(st_b)
            x = _outproj([out_a, out_b], p["w_out0"][e], x, mod_l, tile_row, tm, 512, 2)
        else:
            o = l // 2
            w_in = p["w_in1"][o]
            nz = H_QK * DK_C * 2 + 2 * H_V * DV_C
            z1 = _inproj(x, mod_l, tile_row, p["norm1_g"][l], w_in, 0, nz, ACT_DTYPE, tm, 512, 0, 1)
            ab = _inproj(x, mod_l, tile_row, p["norm1_g"][l], w_in, nz, 4 * H_V, F32, tm, 4 * H_V, 0, 1)
            out_c, st_c = _gdn(z1, ab, bsz, n, p["conv_c"][o], p["gdn_a_log"][o], p["gdn_dt_bias"][o],
                               p["gdn_norm_g"][o], None if s_gdn is None else s_gdn[:, o])
            outs.setdefault("gdn", []).append(st_c)
            x = _outproj([out_c], p["w_out1"][o], x, mod_l, tile_row, tm, 512, 2)
        x = _ec_moe(x, mod_l, req_row, p["norm2_g"][l], p["router_t"][l], p["exp_w1"], p["exp_w3"],
                    p["exp_w2"], l, p["final_g"], bsz, n, last)
    return x.reshape(bsz, n, d), outs


def kernel(x_prompt, x_sample, state_rglru, state_ret, state_gdn, c, c_ctx, mod_w, mod_b, norm1_g, norm2_g, w_in0, conv_a, lru_wa, lru_ba, lru_wi, lru_bi, lru_lam, ret_decay, w_out0, w_in1, conv_c, gdn_a_log, gdn_dt_bias, gdn_norm_g, w_out1, router_w, exp_w1, exp_w3, exp_w2, final_g):
    pad = (-(c.shape[0] + 1)) % 8
    cond = jnp.concatenate([c_ctx[None, :], c, jnp.zeros((pad, c.shape[1]), F32)], axis=0)
    mod = _modulation(cond, mod_w, mod_b)
    p = dict(mod=mod, norm1_g=norm1_g, norm2_g=norm2_g,
             w_in0=w_in0.astype(BF16), conv_a=conv_a, lru_wa=lru_wa, lru_ba=lru_ba, lru_wi=lru_wi,
             lru_bi=lru_bi, lru_lam=lru_lam, ret_decay=ret_decay, w_out0=w_out0.astype(BF16),
             w_in1=w_in1.astype(BF16), conv_c=conv_c, gdn_a_log=gdn_a_log, gdn_dt_bias=gdn_dt_bias,
             gdn_norm_g=gdn_norm_g, w_out1=w_out1.astype(BF16), router_t=jnp.swapaxes(router_w, 1, 2),
             exp_w1=exp_w1, exp_w3=exp_w3, exp_w2=exp_w2, final_g=final_g)
    y_prompt, st = _run_group(x_prompt, mod, None, (None, None, None), p, False)
    y_sample, _ = _run_group(x_sample, mod, 1, (state_rglru, state_ret, state_gdn), p, True)
    return (y_prompt, y_sample, jnp.stack(st["lru"], axis=1), jnp.stack(st["ret"], axis=1),
            jnp.stack(st["gdn"], axis=1))
```

```python
import functools

import jax
import jax.numpy as jnp
from jax import lax
from jax.experimental import pallas as pl
from jax.experimental.pallas import tpu as pltpu

F32 = jnp.float32
BF16 = jnp.bfloat16
I32 = jnp.int32
ACT_DTYPE = BF16
EPS = 1e-6
GRID_W = 64
LRU_C = 8.0
ROPE_BASE = 10000.0
H_A, BLK_A = 16, 128
H_B, DK_B, DV_B = 8, 128, 256
H_QK, H_V, DK_C, DV_C = 16, 32, 128, 128
GDN_C = 64
GDN_GROUP = 4
RET_C = 256
N_EXPERTS = 16
EC_FACTOR = 2
V7X_VMEM_LIMIT = 56 * 1024 * 1024


def _cparams(n_axes):
    return pltpu.CompilerParams(dimension_semantics=("arbitrary",) * n_axes,
                                vmem_limit_bytes=V7X_VMEM_LIMIT)


def _silu(x):
    return x * jax.nn.sigmoid(x)


def _softplus(x):
    return jnp.maximum(x, 0.0) + jnp.log1p(jnp.exp(-jnp.abs(x)))


def _gelu_tanh(x):
    return 0.5 * x * (1.0 + jnp.tanh(0.7978845608028654 * (x + 0.044715 * (x * x * x))))


def _adaln(x, g, shift, scale):
    r = lax.rsqrt(jnp.mean(x * x, axis=-1, keepdims=True) + EPS)
    return x * r * g * (1.0 + scale) + shift


def _dot(a, b):
    return jnp.dot(a, b, preferred_element_type=F32)


def _dot_nt(a, b):
    return lax.dot_general(a, b, (((1,), (1,)), ((), ())), preferred_element_type=F32)


def _dot_tn(a, b):
    return lax.dot_general(a, b, (((0,), (0,)), ((), ())), preferred_element_type=F32)


def _mod_body(c_ref, w_ref, b_ref, o_ref):
    s = _silu(c_ref[...]).astype(BF16)
    o_ref[0] = _dot(s, w_ref[0].astype(BF16)) + b_ref[0]


def _modulation(cond, mod_w, mod_b):
    n_l, d, d6 = mod_w.shape
    r = cond.shape[0]
    tn = 1024
    out = pl.pallas_call(
        _mod_body, grid=(n_l, d6 // tn),
        in_specs=[pl.BlockSpec((r, d), lambda l, j: (0, 0)),
                  pl.BlockSpec((1, d, tn), lambda l, j: (l, 0, j)),
                  pl.BlockSpec((1, 1, tn), lambda l, j: (l, 0, j))],
        out_specs=pl.BlockSpec((1, r, tn), lambda l, j: (l, 0, j)),
        out_shape=jax.ShapeDtypeStruct((n_l, r, d6), F32),
        compiler_params=_cparams(2), name="modulation")(cond, mod_w, mod_b.reshape(n_l, 1, d6))
    return out.reshape(n_l, r, 6, d)


def _inproj_body(x_ref, m_ref, g_ref, w_ref, o_ref, h_scr, *, shift_idx, scale_idx):
    @pl.when(pl.program_id(1) == 0)
    def _():
        h = _adaln(x_ref[...], g_ref[...], m_ref[shift_idx:shift_idx + 1, :], m_ref[scale_idx:scale_idx + 1, :])
        h_scr[...] = h.astype(BF16)
    o_ref[...] = _dot(h_scr[...], w_ref[...]).astype(o_ref.dtype)


def _inproj(x, mod_l, row_fn, g, w, col0, ncols, out_dtype, tm, tn, shift_idx, scale_idx):
    t, d = x.shape
    c0 = col0 // tn
    return pl.pallas_call(
        functools.partial(_inproj_body, shift_idx=shift_idx, scale_idx=scale_idx),
        grid=(t // tm, ncols // tn),
        in_specs=[pl.BlockSpec((tm, d), lambda i, j: (i, 0)),
                  pl.BlockSpec((None, 6, d), lambda i, j: (row_fn(i), 0, 0)),
                  pl.BlockSpec((1, d), lambda i, j: (0, 0)),
                  pl.BlockSpec((d, tn), lambda i, j: (0, c0 + j))],
        out_specs=pl.BlockSpec((tm, tn), lambda i, j: (i, j)),
        out_shape=jax.ShapeDtypeStruct((t, ncols), out_dtype),
        scratch_shapes=[pltpu.VMEM((tm, d), BF16)],
        compiler_params=_cparams(2), name="adaln_inproj")(x, mod_l, g.reshape(1, d), w)


def _outproj_body(*refs, n_in, gate_idx):
    ins = refs[:n_in]
    w_ref, x_ref, m_ref, o_ref = refs[n_in:]
    acc = None
    k0 = 0
    for a in ins:
        kk = a.shape[1]
        part = _dot(a[...], w_ref[k0:k0 + kk, :])
        acc = part if acc is None else acc + part
        k0 += kk
    o_ref[...] = x_ref[...] + m_ref[gate_idx:gate_idx + 1, :] * acc


def _outproj(ins, w, x, mod_l, row_fn, tm, tn, gate_idx):
    t, d = x.shape
    ktot = w.shape[0]
    in_specs = [pl.BlockSpec((tm, a.shape[1]), lambda i, j: (i, 0)) for a in ins]
    in_specs += [pl.BlockSpec((ktot, tn), lambda i, j: (0, j)),
                 pl.BlockSpec((tm, tn), lambda i, j: (i, j)),
                 pl.BlockSpec((None, 6, tn), lambda i, j: (row_fn(i), 0, j))]
    return pl.pallas_call(
        functools.partial(_outproj_body, n_in=len(ins), gate_idx=gate_idx),
        grid=(t // tm, d // tn), in_specs=in_specs,
        out_specs=pl.BlockSpec((tm, tn), lambda i, j: (i, j)),
        out_shape=jax.ShapeDtypeStruct((t, d), F32),
        compiler_params=_cparams(2), name="outproj_residual")(*ins, w, x, mod_l)


def _fill_padded(xpad_ref, x_ref):
    n = x_ref.shape[0]
    zeros = jnp.zeros((8, xpad_ref.shape[1]), F32)
    xpad_ref[0:8, :] = zeros
    xpad_ref[n + 8:n + 16, :] = zeros
    xpad_ref[8:n + 8, :] = x_ref[...].astype(F32)


def _conv_rows(xpad_ref, w, s, rows):
    acc = None
    for k in range(4):
        term = w[k:k + 1, :] * xpad_ref[s + 6 + k:s + 6 + k + rows, :]
        acc = term if acc is None else acc + term
    return acc


def _lru_body(gate_ref, x_ref, cw_ref, wa_ref, wi_ref, ba_ref, bi_ref, lam_ref, h0_ref,
              out_ref, st_ref, xpad, a_f, b_f, a_b, b_b, p_f, h_f, p_b, h_b, *, rows):
    n, cw = x_ref.shape
    nblk = cw // BLK_A
    _fill_padded(xpad, x_ref)
    w = cw_ref[...]
    scr = ((a_f, b_f), (a_b, b_b))
    for s in range(0, n, rows):
        xa = _conv_rows(xpad, w, s, rows)
        xab = xa.astype(BF16)
        for d in range(2):
            r_parts, i_parts = [], []
            for i in range(cw // BLK_A):
                xb = xab[:, i * BLK_A:(i + 1) * BLK_A]
                r_parts.append(_dot(xb, wa_ref[d, i].astype(BF16)))
                i_parts.append(_dot(xb, wi_ref[d, i].astype(BF16)))
            r = jax.nn.sigmoid(jnp.concatenate(r_parts, axis=1) + ba_ref[d:d + 1, :])
            ig = jax.nn.sigmoid(jnp.concatenate(i_parts, axis=1) + bi_ref[d:d + 1, :])
            log_a = (-LRU_C) * r * _softplus(-lam_ref[d:d + 1, :])
            a = jnp.exp(log_a)
            mult = jnp.sqrt(jnp.tanh(-log_a) * (a * a + 1.0))
            b = mult * (ig * xa)
            for i in range(nblk):
                scr[d][0][i, s:s + rows, :] = a[:, i * BLK_A:(i + 1) * BLK_A]
                scr[d][1][i, s:s + rows, :] = b[:, i * BLK_A:(i + 1) * BLK_A]

    seg = n // 8

    def step(i, carry):
        tb = seg - 1 - i
        rf = pl.ds(i, 8, stride=seg)
        rb = pl.ds(tb, 8, stride=seg)
        wf = pl.ds(pl.multiple_of(i * 8, 8), 8)
        wb = pl.ds(pl.multiple_of(tb * 8, 8), 8)
        out = []
        for j in range(nblk):
            pf, hf, pb, hb = carry[4 * j:4 * j + 4]
            af = a_f[j, rf, :]
            ab = a_b[j, rb, :]
            hf = af * hf + b_f[j, rf, :]
            hb = ab * hb + b_b[j, rb, :]
            pf = af * pf
            pb = ab * pb
            p_f[j, wf, :] = pf
            h_f[j, wf, :] = hf
            p_b[j, wb, :] = pb
            h_b[j, wb, :] = hb
            out += [pf, hf, pb, hb]
        return tuple(out)

    ones = jnp.ones((8, BLK_A), F32)
    zeros = jnp.zeros((8, BLK_A), F32)
    lax.fori_loop(0, seg, step, (ones, zeros, ones, zeros) * nblk, unroll=4)

    carry_f = [h0_ref[0, 0:1, j * BLK_A:(j + 1) * BLK_A] for j in range(nblk)]
    for s in range(8):
        sl = slice(s * seg, (s + 1) * seg)
        il = pl.ds(s, seg, stride=8)
        for j in range(nblk):
            h = h_f[j, il, :] + p_f[j, il, :] * carry_f[j]
            b_f[j, sl, :] = h
            carry_f[j] = h[seg - 1:seg, :]
    st_ref[0, 0:1, :] = jnp.concatenate(carry_f, axis=1)
    carry_b = [h0_ref[0, 1:2, j * BLK_A:(j + 1) * BLK_A] for j in range(nblk)]
    for s in range(7, -1, -1):
        sl = slice(s * seg, (s + 1) * seg)
        il = pl.ds(s, seg, stride=8)
        hsum = []
        for j in range(nblk):
            h = h_b[j, il, :] + p_b[j, il, :] * carry_b[j]
            carry_b[j] = h[0:1, :]
            hsum.append(b_f[j, sl, :] + h)
        out_ref[sl, :] = (_gelu_tanh(gate_ref[sl, :]) * jnp.concatenate(hsum, axis=1)).astype(BF16)
    st_ref[0, 1:2, :] = jnp.concatenate(carry_b, axis=1)


def _rglru(z_a, bsz, n, conv_w, wa, wi, ba, bi, lam, h0):
    d_a = conv_w.shape[1]
    cw = 256
    nb = d_a // cw
    hb = cw // BLK_A
    rows = min(n, 256)
    return pl.pallas_call(
        functools.partial(_lru_body, rows=rows), grid=(bsz, nb),
        in_specs=[pl.BlockSpec((n, cw), lambda b, j: (b, j)),
                  pl.BlockSpec((n, cw), lambda b, j: (b, nb + j)),
                  pl.BlockSpec((4, cw), lambda b, j: (0, j)),
                  pl.BlockSpec((2, hb, BLK_A, BLK_A), lambda b, j: (0, j, 0, 0)),
                  pl.BlockSpec((2, hb, BLK_A, BLK_A), lambda b, j: (0, j, 0, 0)),
                  pl.BlockSpec((2, cw), lambda b, j: (0, j)),
                  pl.BlockSpec((2, cw), lambda b, j: (0, j)),
                  pl.BlockSpec((2, cw), lambda b, j: (0, j)),
                  pl.BlockSpec((1, 2, cw), lambda b, j: (b, 0, j))],
        out_specs=[pl.BlockSpec((n, cw), lambda b, j: (b, j)),
                   pl.BlockSpec((1, 2, cw), lambda b, j: (b, 0, j))],
        out_shape=[jax.ShapeDtypeStruct((bsz * n, d_a), BF16),
                   jax.ShapeDtypeStruct((bsz, 2, d_a), F32)],
        scratch_shapes=[pltpu.VMEM((n + 16, cw), F32)] + [pltpu.VMEM((hb, n, BLK_A), F32)] * 8,
        compiler_params=_cparams(2), name="rglru")(z_a, z_a, conv_w, wa, wi, ba, bi, lam, h0)


def _ret_body(*refs, use_rope, has_state, chunk):
    q_ref, k_ref, v_ref, g_ref, dec_ref = refs[:5]
    pos = 5
    if use_rope:
        cos_ref, sin_ref = refs[pos:pos + 2]
        pos += 2
    if has_state:
        s0_ref = refs[pos]
        pos += 1
    o_ref, st_ref, rb_scr = refs[pos:pos + 3]
    n = q_ref.shape[0]
    c_len = chunk
    nc = n // c_len
    head = pl.program_id(1)

    lg = -_softplus(-dec_ref[...])
    lane = lax.broadcasted_iota(I32, lg.shape, 1)
    lgh = jnp.sum(jnp.where(lane == head, lg, 0.0), axis=1, keepdims=True)
    lgf, lgb = lgh[0:1, :], lgh[1:2, :]

    def prep(ref, c, sc):
        x = ref[c * c_len:(c + 1) * c_len, :].astype(F32) * sc
        if use_rope:
            x = (x * cos_ref[c * c_len:(c + 1) * c_len, :]
                 + pltpu.roll(x, DK_B // 2, 1) * sin_ref[c * c_len:(c + 1) * c_len, :])
        return x

    ii = lax.broadcasted_iota(I32, (c_len, 1), 0).astype(F32)
    jj = lax.broadcasted_iota(I32, (1, c_len), 1).astype(F32)
    rel = ii - jj
    dmask = jnp.where(rel > 0.0, jnp.exp(lgf * jnp.maximum(rel, 0.0)),
                      jnp.where(rel < 0.0, jnp.exp(lgb * jnp.maximum(-rel, 0.0)), 2.0))
    qf_dec = jnp.exp(lgf * (ii + 1.0))
    qb_dec = jnp.exp(lgb * (c_len - ii))
    kf_dec = jnp.exp(lgf * (c_len - 1.0 - ii))
    kb_dec = jnp.exp(lgb * ii)
    gf_c = jnp.exp(lgf * c_len)
    gb_c = jnp.exp(lgb * c_len)

    r_b = s0_ref[0, 1, 0] if has_state else jnp.zeros((DK_B, DV_B), F32)
    for c in range(nc - 1, -1, -1):
        rb_scr[c] = r_b
        k = prep(k_ref, c, 1.0)
        v = v_ref[c * c_len:(c + 1) * c_len, :].astype(BF16)
        r_b = r_b * gb_c + _dot_tn((k * kb_dec).astype(BF16), v)
    st_ref[0, 1, 0] = r_b

    r_f = s0_ref[0, 0, 0] if has_state else jnp.zeros((DK_B, DV_B), F32)
    for c in range(nc):
        q = prep(q_ref, c, DK_B ** -0.5)
        k = prep(k_ref, c, 1.0)
        v = v_ref[c * c_len:(c + 1) * c_len, :].astype(BF16)
        s = _dot_nt(q.astype(BF16), k.astype(BF16)) * dmask
        o = _dot(s.astype(BF16), v)
        qd = jnp.concatenate([q * qf_dec, q * qb_dec], axis=1).astype(BF16)
        rcat = jnp.concatenate([r_f, rb_scr[c]], axis=0).astype(BF16)
        o = o + _dot(qd, rcat)
        r_f = r_f * gf_c + _dot_tn((k * kf_dec).astype(BF16), v)
        on = o * lax.rsqrt(jnp.mean(o * o, axis=-1, keepdims=True) + EPS)
        g = g_ref[c * c_len:(c + 1) * c_len, :].astype(F32)
        o_ref[c * c_len:(c + 1) * c_len, :] = (_silu(g) * on).astype(BF16)
    st_ref[0, 0, 0] = r_f


def _retention(z_b, bsz, n, dec, rope, s0):
    nq = H_B
    chunk = min(n, RET_C)
    in_specs = [pl.BlockSpec((n, DK_B), lambda b, h: (b, h)),
                pl.BlockSpec((n, DK_B), lambda b, h: (b, nq + h)),
                pl.BlockSpec((n, DV_B), lambda b, h: (b, nq + h)),
                pl.BlockSpec((n, DV_B), lambda b, h: (b, 2 * nq + h)),
                pl.BlockSpec((2, H_B), lambda b, h: (0, 0))]
    args = [z_b, z_b, z_b, z_b, dec]
    if rope is not None:
        in_specs += [pl.BlockSpec((n, DK_B), lambda b, h: (0, 0))] * 2
        args += list(rope)
    if s0 is not None:
        in_specs.append(pl.BlockSpec((1, 2, 1, DK_B, DV_B), lambda b, h: (b, 0, h, 0, 0)))
        args.append(s0)
    return pl.pallas_call(
        functools.partial(_ret_body, use_rope=rope is not None, has_state=s0 is not None, chunk=chunk),
        grid=(bsz, H_B), in_specs=in_specs,
        out_specs=[pl.BlockSpec((n, DV_B), lambda b, h: (b, h)),
                   pl.BlockSpec((1, 2, 1, DK_B, DV_B), lambda b, h: (b, 0, h, 0, 0))],
        out_shape=[jax.ShapeDtypeStruct((bsz * n, H_B * DV_B), BF16),
                   jax.ShapeDtypeStruct((bsz, 2, H_B, DK_B, DV_B), F32)],
        scratch_shapes=[pltpu.VMEM((n // chunk, DK_B, DV_B), F32)],
        compiler_params=_cparams(2), name="retention")(*args)


def _rope_tables(n):
    rows = n // GRID_W
    n_freq = DK_B // 4
    row = jnp.repeat(jnp.arange(rows, dtype=F32), GRID_W)
    col = (jnp.arange(n) % GRID_W).astype(F32)
    inv = ROPE_BASE ** (-jnp.arange(n_freq, dtype=F32) / n_freq)
    ang = jnp.concatenate([row[:, None] * inv, col[:, None] * inv], axis=-1)
    cos, sin = jnp.cos(ang), jnp.sin(ang)
    return jnp.concatenate([cos, cos], axis=-1), jnp.concatenate([-sin, sin], axis=-1)


def _gdn_body(*refs, has_state, rows):
    (zq_ref, zk_ref, zv_ref, zg_ref, cq_ref, ck_ref, cv_ref, ab_ref, gr_ref,
     a8_ref, dt8_ref, ar_ref, dtr_ref, ng_ref) = refs[:14]
    pos = 14
    if has_state:
        s0_ref = refs[pos]
        pos += 1
    (o_ref, st_ref, xpad, xpad_v, q_scr, k_scr, v_scr, o_scr, bg_scr, grow_scr, s_scr,
     m_scr, blk_scr, u_scr, w_scr, at_scr, qd_scr, kd_scr, eg_scr) = refs[pos:]
    n = zq_ref.shape[0]
    c_len = GDN_C
    c4 = 4 * c_len
    nc = n // c_len
    lg = c_len.bit_length() - 1

    _fill_padded(xpad, zq_ref)
    for s in range(0, n, rows):
        q = _silu(_conv_rows(xpad, cq_ref[...], s, rows))
        q = q * lax.rsqrt(jnp.sum(q * q, axis=-1, keepdims=True) + EPS) * (DK_C ** -0.5)
        q_scr[s:s + rows, :] = q.astype(BF16)
    _fill_padded(xpad, zk_ref)
    for s in range(0, n, rows):
        k = _silu(_conv_rows(xpad, ck_ref[...], s, rows))
        k = k * lax.rsqrt(jnp.sum(k * k, axis=-1, keepdims=True) + EPS)
        k_scr[s:s + rows, :] = k.astype(BF16)
    _fill_padded(xpad_v, zv_ref)
    for s in range(0, n, rows):
        v_scr[s:s + rows, :] = _silu(_conv_rows(xpad_v, cv_ref[...], s, rows))

    ab = ab_ref[0, 0]
    gcols = -jnp.exp(a8_ref[0]) * _softplus(ab + dt8_ref[0])
    w_is_beta = ((lax.broadcasted_iota(I32, ab.shape, 1) >> 1) & 1) == 0
    bg_scr[...] = jnp.where(w_is_beta, jax.nn.sigmoid(ab), gcols)
    grow_scr[...] = -jnp.exp(ar_ref[0]) * _softplus(gr_ref[0, 0] + dtr_ref[0])
    o_scr[...] = jnp.zeros(o_scr.shape, F32)
    for d in range(2):
        if has_state:
            s_scr[d] = jnp.concatenate([s0_ref[0, d, 0], s0_ref[0, d, 1]], axis=0)
        else:
            s_scr[d] = jnp.zeros((2 * DK_C, DV_C), F32)

    ii = lax.broadcasted_iota(I32, (c_len, c4), 0)
    ll = lax.broadcasted_iota(I32, (c_len, c4), 1)
    jj = ll & (c_len - 1)
    lane_stream = ll >> lg
    fwd = lane_stream < 2
    one = lambda m: jnp.where(m, 1.0, 0.0)
    m_scr[0] = jnp.where(fwd, one(jj <= ii), one(jj >= ii))
    m_scr[1] = jnp.where(fwd, one(ii <= jj), one(ii >= jj))
    m_scr[2] = one((ii >> 3) == (jj >> 3))
    n_merge = lg - 3
    for t in range(n_merge):
        sh = 3 + t
        m_scr[3 + t] = jnp.where((ii >> (sh + 1)) == (jj >> (sh + 1)), one((ii >> sh) != (jj >> sh)), 0.0)
    eye = one(ii == jj)
    blk_scr[...] = one((lax.broadcasted_iota(I32, (c4, c4), 0) >> lg)
                       == (lax.broadcasted_iota(I32, (c4, c4), 1) >> lg)).astype(BF16)
    ci = lax.broadcasted_iota(I32, (c_len, c_len), 0)
    cj = lax.broadcasted_iota(I32, (c_len, c_len), 1)
    tri = (one(cj <= ci), one(cj >= ci))
    is_head0 = ((lax.broadcasted_iota(I32, (c4, DK_C), 0) >> lg) & 1) == 0

    def by_head(x):
        return jnp.concatenate([jnp.where(is_head0, x, 0.0), jnp.where(is_head0, 0.0, x)], axis=1).astype(BF16)

    def by_stream(cols):
        return jnp.where(fwd, jnp.where(lane_stream == 0, cols[0], cols[1]),
                         jnp.where(lane_stream == 2, cols[2], cols[3]))

    def block_diag(x):
        return jnp.concatenate([x.astype(BF16)] * 4, axis=0) * blk_scr[...]

    def tri_inverse(a_mats):
        a_d = [a * m_scr[2] for a in a_mats]
        a2 = [_dot(a.astype(BF16), block_diag(a)) for a in a_d]
        ps = [eye - a for a in a_d]
        both = [_dot(jnp.concatenate([p, x], axis=0).astype(BF16), block_diag(x)) for p, x in zip(ps, a2)]
        ps = [p + b[:c_len] for p, b in zip(ps, both)]
        ps = [p + _dot(p.astype(BF16), block_diag(b[c_len:])) for p, b in zip(ps, both)]
        for t in range(n_merge):
            zs = [_dot(p.astype(BF16), block_diag(a * m_scr[3 + t])) for p, a in zip(ps, a_mats)]
            ps = [p - _dot(z.astype(BF16), block_diag(p)) for p, z in zip(ps, zs)]
        return ps

    group = min(GDN_GROUP, nc)

    def prep_group(gi, carry):
        cs = [gi * group + t for t in range(group)]
        rows_c = [pl.ds(pl.multiple_of(c * c_len, c_len), c_len) for c in cs]
        out_rows = [pl.ds(pl.multiple_of(c * c4, c4), c4) for c in cs]
        kc = [k_scr[r, :] for r in rows_c]
        qc = [q_scr[r, :] for r in rows_c]
        kc4 = [jnp.concatenate([k] * 4, axis=0) for k in kc]
        qc4 = [jnp.concatenate([q] * 4, axis=0) for q in qc]
        gram = [_dot_nt(jnp.concatenate([k, q], axis=0), k4) for k, q, k4 in zip(kc, qc, kc4)]
        kk = [g[:c_len] for g in gram]
        qk = [g[c_len:] for g in gram]
        incl = m_scr[0]
        beta, gc_col, g_last, decay, a_mats = [], [], [], [], []
        for t in range(group):
            bg = bg_scr[rows_c[t], :]
            beta_s = [bg[:, 0:1], bg[:, 1:2], bg[:, 4:5], bg[:, 5:6]]
            gcol_s = [bg[:, 2:3], bg[:, 3:4], bg[:, 6:7], bg[:, 7:8]]
            grow = grow_scr[pl.ds(cs[t], 1), :]
            gc_s = [jnp.sum(tri[s // 2] * grow[:, s * c_len:(s + 1) * c_len], axis=1, keepdims=True)
                    for s in range(4)]
            gc_row = jnp.sum(m_scr[1] * by_stream(gcol_s), axis=0, keepdims=True)
            decay.append(jnp.exp((by_stream(gc_s) - gc_row) * incl) * incl)
            a_mats.append((by_stream(beta_s) * kk[t] * decay[t]) * (incl - eye))
            gl_s = [gc_s[s][c_len - 1:c_len, :] if s < 2 else gc_s[s][0:1, :] for s in range(4)]
            beta.append(jnp.concatenate(beta_s, axis=0))
            gc_col.append(jnp.concatenate(gc_s, axis=0))
            g_last.append(jnp.concatenate([jnp.broadcast_to(g, (c_len, 1)) for g in gl_s], axis=0))
        t_inv = tri_inverse(a_mats)
        blk2 = blk_scr[0:2 * c_len, 0:2 * c_len]
        for t in range(group):
            kf = kc4[t].astype(F32)
            qf = qc4[t].astype(F32)
            egc = jnp.exp(gc_col[t])
            vc = v_scr[rows_c[t], :]
            v4 = jnp.concatenate([vc[:, :DV_C], vc[:, DV_C:], vc[:, :DV_C], vc[:, DV_C:]], axis=0)
            kb = kf * (beta[t] * egc)
            rhs = jnp.concatenate([v4 * beta[t], kb], axis=1).astype(BF16)
            uw = _dot(block_diag(t_inv[t]), rhs)
            u_scr[out_rows[t], :] = uw[:, :DV_C]
            w = uw[:, DV_C:]
            w_scr[out_rows[t], :] = by_head(w)
            attn = (qk[t] * decay[t]).astype(BF16)
            at_scr[out_rows[t], :] = jnp.concatenate(
                [jnp.concatenate([attn[:, :2 * c_len]] * 2, axis=0) * blk2,
                 jnp.concatenate([attn[:, 2 * c_len:]] * 2, axis=0) * blk2], axis=0)
            qe = qf * egc
            qd_scr[out_rows[t], :] = by_head(qe)
            ke = kf * jnp.exp(g_last[t] - gc_col[t])
            kd_scr[out_rows[t], :] = by_head(ke)
            egl = jnp.exp(g_last[t])
            for s in range(4):
                eg_scr[pl.ds(cs[t] * 4 + s, 1), :] = jnp.broadcast_to(egl[s * c_len:s * c_len + 1, :], (1, DV_C))
        return carry

    lax.fori_loop(0, nc // group, prep_group, 0)

    def scan_chunk(c, carry):
        ccs = [c, nc - 1 - c]
        rows_d = [pl.ds(pl.multiple_of(ccs[d] * c4 + d * 2 * c_len, 2 * c_len), 2 * c_len) for d in range(2)]
        s_cat = [s_scr[d] for d in range(2)]
        sb = [s.astype(BF16) for s in s_cat]
        ws = [_dot(w_scr[rows_d[d], :], sb[d]) for d in range(2)]
        qs = [_dot(qd_scr[rows_d[d], :], sb[d]) for d in range(2)]
        vnb = [(u_scr[rows_d[d], :] - ws[d]).astype(BF16) for d in range(2)]
        kv = [_dot_tn(kd_scr[rows_d[d], :], vnb[d]) for d in range(2)]
        av = [_dot(at_scr[rows_d[d], :], vnb[d]) for d in range(2)]
        for d in range(2):
            e0 = jnp.broadcast_to(eg_scr[pl.ds(ccs[d] * 4 + 2 * d, 1), :], (DK_C, DV_C))
            e1 = jnp.broadcast_to(eg_scr[pl.ds(ccs[d] * 4 + 2 * d + 1, 1), :], (DK_C, DV_C))
            s_scr[d] = s_cat[d] * jnp.concatenate([e0, e1], axis=0) + kv[d]
            o = qs[d] + av[d]
            tok = pl.ds(pl.multiple_of(ccs[d] * c_len, c_len), c_len)
            o_scr[tok, 0:DV_C] += o[:c_len]
            o_scr[tok, DV_C:2 * DV_C] += o[c_len:]
        return carry

    lax.fori_loop(0, nc, scan_chunk, 0)

    for d in range(2):
        for r in range(2):
            st_ref[0, d, r] = s_scr[d, r * DK_C:(r + 1) * DK_C, :]
    ng = ng_ref[...]
    for s in range(0, n, rows):
        zg = zg_ref[s:s + rows, :].astype(F32)
        for r in range(2):
            o = o_scr[s:s + rows, r * DV_C:(r + 1) * DV_C]
            on = o * lax.rsqrt(jnp.mean(o * o, axis=-1, keepdims=True) + EPS) * ng
            o_ref[s:s + rows, r * DV_C:(r + 1) * DV_C] = (
                on * _silu(zg[:, r * DV_C:(r + 1) * DV_C])).astype(BF16)


def _gdn(z1, ab, bsz, n, conv_c, a_log, dt_bias, norm_g, s0):
    nc = n // GDN_C
    rows = min(n, 256)
    nqb = H_QK
    ab6 = ab.reshape(bsz, n, 2, 2, H_QK, 2)
    ab8 = ab6.transpose(0, 4, 1, 2, 3, 5).reshape(bsz, H_QK, n, 8)
    g_rows = ab6[:, :, :, 1].reshape(bsz, nc, GDN_C, 2, H_QK, 2).transpose(0, 4, 1, 3, 5, 2)
    g_rows = g_rows.reshape(bsz, H_QK, nc, 4 * GDN_C)

    def per_pair(p):
        p3 = p.astype(F32).reshape(2, H_QK, 2).transpose(1, 0, 2)
        return jnp.broadcast_to(p3[:, :, None, :], (H_QK, 2, 2, 2)).reshape(H_QK, 8)

    def per_stream(p):
        p3 = p.astype(F32).reshape(2, H_QK, 2).transpose(1, 0, 2).reshape(H_QK, 4)
        return jnp.repeat(p3, GDN_C, axis=1).reshape(H_QK, 1, 4 * GDN_C)

    a8, dt8 = per_pair(a_log), per_pair(dt_bias)
    in_specs = [pl.BlockSpec((n, DK_C), lambda b, j: (b, j)),
                pl.BlockSpec((n, DK_C), lambda b, j: (b, nqb + j)),
                pl.BlockSpec((n, 2 * DV_C), lambda b, j: (b, nqb + j)),
                pl.BlockSpec((n, 2 * DV_C), lambda b, j: (b, 2 * nqb + j)),
                pl.BlockSpec((4, DK_C), lambda b, j: (0, j)),
                pl.BlockSpec((4, DK_C), lambda b, j: (0, nqb + j)),
                pl.BlockSpec((4, 2 * DV_C), lambda b, j: (0, nqb + j)),
                pl.BlockSpec((1, 1, n, 8), lambda b, j: (b, j, 0, 0)),
                pl.BlockSpec((1, 1, nc, 4 * GDN_C), lambda b, j: (b, j, 0, 0)),
                pl.BlockSpec((1, 1, 8), lambda b, j: (j, 0, 0)),
                pl.BlockSpec((1, 1, 8), lambda b, j: (j, 0, 0)),
                pl.BlockSpec((1, 1, 4 * GDN_C), lambda b, j: (j, 0, 0)),
                pl.BlockSpec((1, 1, 4 * GDN_C), lambda b, j: (j, 0, 0)),
                pl.BlockSpec((1, DV_C), lambda b, j: (0, 0))]
    args = [z1, z1, z1, z1, conv_c, conv_c, conv_c, ab8, g_rows,
            a8.reshape(H_QK, 1, 8), dt8.reshape(H_QK, 1, 8),
            per_stream(a_log), per_stream(dt_bias), norm_g.reshape(1, DV_C)]
    if s0 is not None:
        in_specs.append(pl.BlockSpec((1, 2, 2, DK_C, DV_C), lambda b, j: (b, 0, j, 0, 0)))
        args.append(s0)
    return pl.pallas_call(
        functools.partial(_gdn_body, has_state=s0 is not None, rows=rows),
        grid=(bsz, H_QK), in_specs=in_specs,
        out_specs=[pl.BlockSpec((n, 2 * DV_C), lambda b, j: (b, j)),
                   pl.BlockSpec((1, 2, 2, DK_C, DV_C), lambda b, j: (b, 0, j, 0, 0))],
        out_shape=[jax.ShapeDtypeStruct((bsz * n, H_V * DV_C), BF16),
                   jax.ShapeDtypeStruct((bsz, 2, H_V, DK_C, DV_C), F32)],
        scratch_shapes=[pltpu.VMEM((n + 16, DK_C), F32), pltpu.VMEM((n + 16, 2 * DV_C), F32),
                        pltpu.VMEM((n, DK_C), BF16), pltpu.VMEM((n, DK_C), BF16),
                        pltpu.VMEM((n, 2 * DV_C), F32), pltpu.VMEM((n, 2 * DV_C), F32),
                        pltpu.VMEM((n, 8), F32), pltpu.VMEM((nc, 4 * GDN_C), F32),
                        pltpu.VMEM((2, 2 * DK_C, DV_C), F32),
                        pltpu.VMEM((GDN_C.bit_length() - 1, GDN_C, 4 * GDN_C), F32),
                        pltpu.VMEM((4 * GDN_C, 4 * GDN_C), BF16),
                        pltpu.VMEM((4 * n, DV_C), F32), pltpu.VMEM((4 * n, 2 * DK_C), BF16),
                        pltpu.VMEM((4 * n, 2 * GDN_C), BF16), pltpu.VMEM((4 * n, 2 * DK_C), BF16),
                        pltpu.VMEM((4 * n, 2 * DK_C), BF16), pltpu.VMEM((4 * nc, DV_C), F32)],
        compiler_params=_cparams(2), name="gated_deltanet")(*args)


def _router_body(x_ref, m_ref, g_ref, rwt_ref, h_ref, aff_ref):
    h = _adaln(x_ref[...], g_ref[...], m_ref[3:4, :], m_ref[4:5, :])
    h_ref[...] = h.astype(BF16)
    logits = _dot_nt(rwt_ref[...].astype(BF16), h.astype(BF16))
    ex = jnp.exp(logits - jnp.max(logits, axis=0, keepdims=True))
    aff_ref[0] = ex / jnp.sum(ex, axis=0, keepdims=True)


def _router(x, mod_l, row_fn, g, rw_t, bsz, n, tt):
    t, d = x.shape
    e = rw_t.shape[0]
    nt = n // tt
    return pl.pallas_call(
        _router_body, grid=(bsz, nt),
        in_specs=[pl.BlockSpec((tt, d), lambda b, i: (b * nt + i, 0)),
                  pl.BlockSpec((None, 6, d), lambda b, i: (row_fn(b), 0, 0)),
                  pl.BlockSpec((1, d), lambda b, i: (0, 0)),
                  pl.BlockSpec((e, d), lambda b, i: (0, 0))],
        out_specs=[pl.BlockSpec((tt, d), lambda b, i: (b * nt + i, 0)),
                   pl.BlockSpec((1, e, tt), lambda b, i: (b, 0, i))],
        out_shape=[jax.ShapeDtypeStruct((t, d), BF16), jax.ShapeDtypeStruct((bsz, e, n), F32)],
        compiler_params=_cparams(2), name="adaln_router")(x, mod_l, g.reshape(1, d), rw_t)


def _cumsum_lanes(m):
    r, n = m.shape
    blk = min(n, 256)
    tri = jnp.where(lax.broadcasted_iota(I32, (blk, blk), 0) <= lax.broadcasted_iota(I32, (blk, blk), 1),
                    1.0, 0.0).astype(BF16)
    outs = []
    run = jnp.zeros((r, 1), F32)
    for c in range(n // blk):
        loc = _dot(m[:, c * blk:(c + 1) * blk], tri) + run
        outs.append(loc)
        run = loc[:, blk - 1:blk]
    return outs[0] if len(outs) == 1 else jnp.concatenate(outs, axis=1)


def _topk_body(aff_ref, pos_ref, *, cap):
    bb, e, n = aff_ref.shape
    rows = bb * e
    bits = pltpu.bitcast(aff_ref[...].reshape(rows, n), I32)
    capf = float(cap)
    prefix = jnp.zeros((rows, 1), I32)
    for bit in range(30, -1, -1):
        cand = prefix | (1 << bit)
        cnt = jnp.sum(jnp.where(bits >= cand, 1.0, 0.0), axis=1, keepdims=True)
        prefix = jnp.where(cnt >= capf, cand, prefix)
    gt = bits > prefix
    eq = bits == prefix
    need = capf - jnp.sum(jnp.where(gt, 1.0, 0.0), axis=1, keepdims=True)
    eq_rank = _cumsum_lanes(jnp.where(eq, 1.0, 0.0).astype(BF16))
    sel = jnp.where(gt, 1.0, jnp.where(eq, jnp.where(eq_rank <= need, 1.0, 0.0), 0.0))
    slot = _cumsum_lanes(sel.astype(BF16)) - 1.0
    pos_ref[...] = jnp.where(sel > 0.5, slot, -1.0).astype(I32).reshape(bb, e, n)


def _topk(aff, cap):
    bsz, e, n = aff.shape
    return pl.pallas_call(
        functools.partial(_topk_body, cap=cap), grid=(1,),
        in_specs=[pl.BlockSpec((bsz, e, n), lambda i: (0, 0, 0))],
        out_specs=pl.BlockSpec((bsz, e, n), lambda i: (0, 0, 0)),
        out_shape=jax.ShapeDtypeStruct((bsz, e, n), I32),
        compiler_params=_cparams(1), name="expert_choice_topk")(aff)


def _gather_body(h_ref, pos_ref, aff_ref, xg_ref, gate_ref):
    eg, cap, _ = xg_ref.shape
    n = h_ref.shape[0]
    slot = lax.broadcasted_iota(I32, (cap, n), 0)
    for e in range(eg):
        hit = slot == pos_ref[0, e]
        onehot = jnp.where(hit, 1.0, 0.0).astype(BF16)
        xg_ref[e] = _dot(onehot, h_ref[...]).astype(BF16)
        gate_ref[e] = jnp.sum(jnp.where(hit, aff_ref[0, e], 0.0), axis=1, keepdims=True)


def _gather(h, pos4, aff4, bsz, n, cap, eg):
    d = h.shape[1]
    e = pos4.shape[1]
    return pl.pallas_call(
        _gather_body, grid=(bsz, e // eg),
        in_specs=[pl.BlockSpec((n, d), lambda b, g: (b, 0)),
                  pl.BlockSpec((1, eg, 1, n), lambda b, g: (b, g, 0, 0)),
                  pl.BlockSpec((1, eg, 1, n), lambda b, g: (b, g, 0, 0))],
        out_specs=[pl.BlockSpec((eg, cap, d), lambda b, g: (g, b, 0)),
                   pl.BlockSpec((eg, cap, 1), lambda b, g: (g, b, 0))],
        out_shape=[jax.ShapeDtypeStruct((e, bsz * cap, d), BF16),
                   jax.ShapeDtypeStruct((e, bsz * cap, 1), F32)],
        compiler_params=_cparams(2), name="moe_gather")(h, pos4, aff4)


def _ffn_body(x_ref, gate_ref, w1_ref, w3_ref, w2_ref, o_ref, acc):
    f = pl.program_id(2)
    x = x_ref[0]
    a = _dot(x, w1_ref[0].astype(BF16))
    b = _dot(x, w3_ref[0].astype(BF16))
    part = _dot((_silu(a) * b).astype(BF16), w2_ref[0].astype(BF16))

    @pl.when(f == 0)
    def _():
        acc[...] = part

    @pl.when(f > 0)
    def _():
        acc[...] += part

    @pl.when(f == pl.num_programs(2) - 1)
    def _():
        o_ref[0] = (acc[...] * gate_ref[0]).astype(BF16)


def _ffn(xg, gate, w1, w3, w2, layer, tm, tf):
    e, m, d = xg.shape
    ff = w1.shape[3]
    return pl.pallas_call(
        _ffn_body, grid=(e, m // tm, ff // tf),
        in_specs=[pl.BlockSpec((1, tm, d), lambda x, i, f: (x, i, 0)),
                  pl.BlockSpec((1, tm, 1), lambda x, i, f: (x, i, 0)),
                  pl.BlockSpec((None, 1, d, tf), lambda x, i, f: (layer, x, 0, f)),
                  pl.BlockSpec((None, 1, d, tf), lambda x, i, f: (layer, x, 0, f)),
                  pl.BlockSpec((None, 1, tf, d), lambda x, i, f: (layer, x, f, 0))],
        out_specs=pl.BlockSpec((1, tm, d), lambda x, i, f: (x, i, 0)),
        out_shape=jax.ShapeDtypeStruct((e, m, d), BF16),
        scratch_shapes=[pltpu.VMEM((tm, d), F32)],
        compiler_params=_cparams(3), name="expert_ffn")(xg, gate, w1, w3, w2)


def _combine_body(pos_ref, y_ref, x_ref, m_ref, fg_ref, o_ref, acc, *, final):
    e = pl.program_id(2)
    eg, cap, d = y_ref.shape
    tt = x_ref.shape[0]
    rows8 = [pos_ref[0, g] for g in range(eg)] + [jnp.zeros((1, tt), I32)] * ((-eg) % 8)
    pos_rows = jnp.concatenate(rows8, axis=0).astype(F32).astype(BF16)
    eye = jnp.where(lax.broadcasted_iota(I32, (tt, tt), 0) == lax.broadcasted_iota(I32, (tt, tt), 1),
                    1.0, 0.0).astype(BF16)
    pos_cols = _dot_nt(eye, pos_rows)
    slot = lax.broadcasted_iota(I32, (tt, cap), 1).astype(F32)
    onehot = jnp.concatenate([jnp.where(slot == pos_cols[:, g:g + 1], 1.0, 0.0).astype(BF16)
                              for g in range(eg)], axis=1)
    part = _dot(onehot, y_ref[...].reshape(eg * cap, d))

    @pl.when(e == 0)
    def _():
        acc[...] = part

    @pl.when(e > 0)
    def _():
        acc[...] += part

    @pl.when(e == pl.num_programs(2) - 1)
    def _():
        xn = x_ref[...] + m_ref[5:6, :] * acc[...]
        if final:
            xn = xn * lax.rsqrt(jnp.mean(xn * xn, axis=-1, keepdims=True) + EPS) * fg_ref[...]
        o_ref[...] = xn


def _combine(pos4, y, x, mod_l, row_fn, final_g, bsz, n, cap, tt, eg, final):
    t, d = x.shape
    e = y.shape[0]
    nt = n // tt
    return pl.pallas_call(
        functools.partial(_combine_body, final=final), grid=(bsz, nt, e // eg),
        in_specs=[pl.BlockSpec((1, eg, 1, tt), lambda b, i, x_: (b, x_, 0, i)),
                  pl.BlockSpec((eg, cap, d), lambda b, i, x_: (x_, b, 0)),
                  pl.BlockSpec((tt, d), lambda b, i, x_: (b * nt + i, 0)),
                  pl.BlockSpec((None, 6, d), lambda b, i, x_: (row_fn(b), 0, 0)),
                  pl.BlockSpec((1, d), lambda b, i, x_: (0, 0))],
        out_specs=pl.BlockSpec((tt, d), lambda b, i, x_: (b * nt + i, 0)),
        out_shape=jax.ShapeDtypeStruct((t, d), F32),
        scratch_shapes=[pltpu.VMEM((tt, d), F32)],
        compiler_params=_cparams(3), name="moe_combine")(pos4, y, x, mod_l, final_g.reshape(1, d))


def _ec_moe(x, mod_l, row_fn, norm_g, rw_t, w1, w3, w2, layer, final_g, bsz, n, final):
    e = rw_t.shape[0]
    cap = EC_FACTOR * n // e
    tt = min(n, 512)
    h, aff = _router(x, mod_l, row_fn, norm_g, rw_t, bsz, n, tt)
    pos = _topk(aff, cap)
    pos4 = pos.reshape(bsz, e, 1, n)
    eg = e if cap * e <= 1024 else 1
    xg, gate = _gather(h, pos4, aff.reshape(bsz, e, 1, n), bsz, n, cap, eg)
    m = bsz * cap
    tm = m if m <= 1024 else 1024
    y = _ffn(xg, gate, w1, w3, w2, layer, tm, 256)
    eg_c = max(1, min(e, 1024 // cap))
    return _combine(pos4, y, x, mod_l, row_fn, final_g, bsz, n, cap, tt, eg_c, final)


def _run_group(x3, mod, row_of_request, states, p, with_rope):
    bsz, n, d = x3.shape
    x = x3.reshape(bsz * n, d)
    tm = min(n, 1024) if row_of_request is not None else min(bsz * n, 1024)
    if row_of_request is None:
        tile_row = lambda i: 0
        req_row = lambda b: 0
    else:
        tiles_per_req = n // tm
        tile_row = lambda i: row_of_request + i // tiles_per_req
        req_row = lambda b: row_of_request + b
    s_lru, s_ret, s_gdn = states
    outs = {}
    depth = p["mod"].shape[0]
    for l in range(depth):
        mod_l = mod[l]
        last = l == depth - 1
        if l % 2 == 0:
            e = l // 2
            w_in = p["w_in0"][e]
            d_a = p["conv_a"].shape[-1]
            z_a = _inproj(x, mod_l, tile_row, p["norm1_g"][l], w_in, 0, 2 * d_a, F32, tm, 512, 0, 1)
            z_b = _inproj(x, mod_l, tile_row, p["norm1_g"][l], w_in, 2 * d_a, w_in.shape[1] - 2 * d_a,
                          ACT_DTYPE, tm, 512, 0, 1)
            h0 = jnp.zeros((bsz, 2, d_a), F32) if s_lru is None else s_lru[:, e]
            out_a, st_a = _rglru(z_a, bsz, n, p["conv_a"][e], p["lru_wa"][e], p["lru_wi"][e],
                                 p["lru_ba"][e], p["lru_bi"][e], p["lru_lam"][e], h0)
            out_b, st_b = _retention(z_b, bsz, n, p["ret_decay"][e], _rope_tables(n) if with_rope else None,
                                     None if s_ret is None else s_ret[:, e])
            outs.setdefault("lru", []).append(st_a)
            outs.setdefault("ret", []).append(st_b)
            x = _outproj([out_a, out_b], p["w_out0"][e], x, mod_l, tile_row, tm, 512, 2)
        else:
            o = l // 2
            w_in = p["w_in1"][o]
            nz = H_QK * DK_C * 2 + 2 * H_V * DV_C
            z1 = _inproj(x, mod_l, tile_row, p["norm1_g"][l], w_in, 0, nz, ACT_DTYPE, tm, 512, 0, 1)
            ab = _inproj(x, mod_l, tile_row, p["norm1_g"][l], w_in, nz, 4 * H_V, F32, tm, 4 * H_V, 0, 1)
            out_c, st_c = _gdn(z1, ab, bsz, n, p["conv_c"][o], p["gdn_a_log"][o], p["gdn_dt_bias"][o],
                               p["gdn_norm_g"][o], None if s_gdn is None else s_gdn[:, o])
            outs.setdefault("gdn", []).append(st_c)
            x = _outproj([out_c], p["w_out1"][o], x, mod_l, tile_row, tm, 512, 2)
        x = _ec_moe(x, mod_l, req_row, p["norm2_g"][l], p["router_t"][l], p["exp_w1"], p["exp_w3"],
                    p["exp_w2"], l, p["final_g"], bsz, n, last)
    return x.reshape(bsz, n, d), outs


def kernel(x_prompt, x_sample, state_rglru, state_ret, state_gdn, c, c_ctx, mod_w, mod_b, norm1_g, norm2_g, w_in0, conv_a, lru_wa, lru_ba, lru_wi, lru_bi, lru_lam, ret_decay, w_out0, w_in1, conv_c, gdn_a_log, gdn_dt_bias, gdn_norm_g, w_out1, router_w, exp_w1, exp_w3, exp_w2, final_g):
    pad = (-(c.shape[0] + 1)) % 8
    cond = jnp.concatenate([c_ctx[None, :], c, jnp.zeros((pad, c.shape[1]), F32)], axis=0)
    mod = _modulation(cond, mod_w, mod_b)
    p = dict(mod=mod, norm1_g=norm1_g, norm2_g=norm2_g,
             w_in0=w_in0.astype(BF16), conv_a=conv_a, lru_wa=lru_wa, lru_ba=lru_ba, lru_wi=lru_wi,
             lru_bi=lru_bi, lru_lam=lru_lam, ret_decay=ret_decay, w_out0=w_out0.astype(BF16),
             w_in1=w_in1.astype(BF16), conv_c=conv_c, gdn_a_log=gdn_a_log, gdn_dt_bias=gdn_dt_bias,
             gdn_norm_g=gdn_norm_g, w_out1=w_out1.astype(BF16), router_t=jnp.swapaxes(router_w, 1, 2),
             exp_w1=exp_w1, exp_w3=exp_w3, exp_w2=exp_w2, final_g=final_g)
    y_prompt, st = _run_group(x_prompt, mod, None, (None, None, None), p, False)
    y_sample, _ = _run_group(x_sample, mod, 1, (state_rglru, state_ret, state_gdn), p, True)
    return (y_prompt, y_sample, jnp.stack(st["lru"], axis=1), jnp.stack(st["ret"], axis=1),
            jnp.stack(st["gdn"], axis=1))
```

```python
import functools

import jax
import jax.numpy as jnp
from jax import lax
from jax.experimental import pallas as pl
from jax.experimental.pallas import tpu as pltpu

F32 = jnp.float32
BF16 = jnp.bfloat16
I32 = jnp.int32
ACT_DTYPE = BF16
EPS = 1e-6
GRID_W = 64
LRU_C = 8.0
ROPE_BASE = 10000.0
H_A, BLK_A = 16, 128
H_B, DK_B, DV_B = 8, 128, 256
H_QK, H_V, DK_C, DV_C = 16, 32, 128, 128
GDN_C = 64
GDN_GROUP = 4
RET_C = 256
N_EXPERTS = 16
EC_FACTOR = 2
V7X_VMEM_LIMIT = 56 * 1024 * 1024


def _cparams(n_axes):
    return pltpu.CompilerParams(dimension_semantics=("arbitrary",) * n_axes,
                                vmem_limit_bytes=V7X_VMEM_LIMIT)


def _silu(x):
    return x * jax.nn.sigmoid(x)


def _softplus(x):
    return jnp.maximum(x, 0.0) + jnp.log1p(jnp.exp(-jnp.abs(x)))


def _gelu_tanh(x):
    return 0.5 * x * (1.0 + jnp.tanh(0.7978845608028654 * (x + 0.044715 * (x * x * x))))


def _adaln(x, g, shift, scale):
    r = lax.rsqrt(jnp.mean(x * x, axis=-1, keepdims=True) + EPS)
    return x * r * g * (1.0 + scale) + shift


def _dot(a, b):
    return jnp.dot(a, b, preferred_element_type=F32)


def _dot_nt(a, b):
    return lax.dot_general(a, b, (((1,), (1,)), ((), ())), preferred_element_type=F32)


def _dot_tn(a, b):
    return lax.dot_general(a, b, (((0,), (0,)), ((), ())), preferred_element_type=F32)


def _mod_body(c_ref, w_ref, b_ref, o_ref):
    s = _silu(c_ref[...]).astype(BF16)
    o_ref[0] = _dot(s, w_ref[0].astype(BF16)) + b_ref[0]


def _modulation(cond, mod_w, mod_b):
    n_l, d, d6 = mod_w.shape
    r = cond.shape[0]
    tn = 1024
    out = pl.pallas_call(
        _mod_body, grid=(n_l, d6 // tn),
        in_specs=[pl.BlockSpec((r, d), lambda l, j: (0, 0)),
                  pl.BlockSpec((1, d, tn), lambda l, j: (l, 0, j)),
                  pl.BlockSpec((1, 1, tn), lambda l, j: (l, 0, j))],
        out_specs=pl.BlockSpec((1, r, tn), lambda l, j: (l, 0, j)),
        out_shape=jax.ShapeDtypeStruct((n_l, r, d6), F32),
        compiler_params=_cparams(2), name="modulation")(cond, mod_w, mod_b.reshape(n_l, 1, d6))
    return out.reshape(n_l, r, 6, d)


def _inproj_body(x_ref, m_ref, g_ref, w_ref, o_ref, h_scr, *, shift_idx, scale_idx):
    @pl.when(pl.program_id(1) == 0)
    def _():
        h = _adaln(x_ref[...], g_ref[...], m_ref[shift_idx:shift_idx + 1, :], m_ref[scale_idx:scale_idx + 1, :])
        h_scr[...] = h.astype(BF16)
    o_ref[...] = _dot(h_scr[...], w_ref[...]).astype(o_ref.dtype)


def _inproj(x, mod_l, row_fn, g, w, col0, ncols, out_dtype, tm, tn, shift_idx, scale_idx):
    t, d = x.shape
    c0 = col0 // tn
    return pl.pallas_call(
        functools.partial(_inproj_body, shift_idx=shift_idx, scale_idx=scale_idx),
        grid=(t // tm, ncols // tn),
        in_specs=[pl.BlockSpec((tm, d), lambda i, j: (i, 0)),
                  pl.BlockSpec((None, 6, d), lambda i, j: (row_fn(i), 0, 0)),
                  pl.BlockSpec((1, d), lambda i, j: (0, 0)),
                  pl.BlockSpec((d, tn), lambda i, j: (0, c0 + j))],
        out_specs=pl.BlockSpec((tm, tn), lambda i, j: (i, j)),
        out_shape=jax.ShapeDtypeStruct((t, ncols), out_dtype),
        scratch_shapes=[pltpu.VMEM((tm, d), BF16)],
        compiler_params=_cparams(2), name="adaln_inproj")(x, mod_l, g.reshape(1, d), w)


def _outproj_body(*refs, n_in, gate_idx):
    ins = refs[:n_in]
    w_ref, x_ref, m_ref, o_ref = refs[n_in:]
    acc = None
    k0 = 0
    for a in ins:
        kk = a.shape[1]
        part = _dot(a[...], w_ref[k0:k0 + kk, :])
        acc = part if acc is None else acc + part
        k0 += kk
    o_ref[...] = x_ref[...] + m_ref[gate_idx:gate_idx + 1, :] * acc


def _outproj(ins, w, x, mod_l, row_fn, tm, tn, gate_idx):
    t, d = x.shape
    ktot = w.shape[0]
    in_specs = [pl.BlockSpec((tm, a.shape[1]), lambda i, j: (i, 0)) for a in ins]
    in_specs += [pl.BlockSpec((ktot, tn), lambda i, j: (0, j)),
                 pl.BlockSpec((tm, tn), lambda i, j: (i, j)),
                 pl.BlockSpec((None, 6, tn), lambda i, j: (row_fn(i), 0, j))]
    return pl.pallas_call(
        functools.partial(_outproj_body, n_in=len(ins), gate_idx=gate_idx),
        grid=(t // tm, d // tn), in_specs=in_specs,
        out_specs=pl.BlockSpec((tm, tn), lambda i, j: (i, j)),
        out_shape=jax.ShapeDtypeStruct((t, d), F32),
        compiler_params=_cparams(2), name="outproj_residual")(*ins, w, x, mod_l)


def _fill_padded(xpad_ref, x_ref):
    n = x_ref.shape[0]
    zeros = jnp.zeros((8, xpad_ref.shape[1]), F32)
    xpad_ref[0:8, :] = zeros
    xpad_ref[n + 8:n + 16, :] = zeros
    xpad_ref[8:n + 8, :] = x_ref[...].astype(F32)


def _conv_rows(xpad_ref, w, s, rows):
    acc = None
    for k in range(4):
        term = w[k:k + 1, :] * xpad_ref[s + 6 + k:s + 6 + k + rows, :]
        acc = term if acc is None else acc + term
    return acc


def _lru_body(gate_ref, x_ref, cw_ref, wa_ref, wi_ref, ba_ref, bi_ref, lam_ref, h0_ref,
              out_ref, st_ref, xpad, a_f, b_f, a_b, b_b, *, rows):
    n, cw = x_ref.shape
    nblk = cw // BLK_A
    _fill_padded(xpad, x_ref)
    w = cw_ref[...]
    scr = ((a_f, b_f), (a_b, b_b))
    for s in range(0, n, rows):
        xa = _conv_rows(xpad, w, s, rows)
        xab = xa.astype(BF16)
        for d in range(2):
            r_parts, i_parts = [], []
            for i in range(cw // BLK_A):
                xb = xab[:, i * BLK_A:(i + 1) * BLK_A]
                r_parts.append(_dot(xb, wa_ref[d, i].astype(BF16)))
                i_parts.append(_dot(xb, wi_ref[d, i].astype(BF16)))
            r = jax.nn.sigmoid(jnp.concatenate(r_parts, axis=1) + ba_ref[d:d + 1, :])
            ig = jax.nn.sigmoid(jnp.concatenate(i_parts, axis=1) + bi_ref[d:d + 1, :])
            log_a = (-LRU_C) * r * _softplus(-lam_ref[d:d + 1, :])
            a = jnp.exp(log_a)
            mult = jnp.sqrt(jnp.tanh(-log_a) * (a * a + 1.0))
            b = mult * (ig * xa)
            for i in range(nblk):
                scr[d][0][i, s:s + rows, :] = a[:, i * BLK_A:(i + 1) * BLK_A]
                scr[d][1][i, s:s + rows, :] = b[:, i * BLK_A:(i + 1) * BLK_A]

    ri = lax.broadcasted_iota(I32, (8, BLK_A), 0)
    nt = n // 8

    def tile_scan(a, b, carry, with_rows):
        for d in (1, 2, 4):
            keep = (ri >= d) if with_rows else (ri < 8 - d)
            shift = d if with_rows else 8 - d
            b = b + a * jnp.where(keep, pltpu.roll(b, shift, 0), 0.0)
            a = a * jnp.where(keep, pltpu.roll(a, shift, 0), 1.0)
        return b + a * carry

    def step(t, carry):
        tf = pl.ds(pl.multiple_of(t * 8, 8), 8)
        tb = pl.ds(pl.multiple_of((nt - 1 - t) * 8, 8), 8)
        out = []
        for j in range(nblk):
            hf = tile_scan(a_f[j, tf, :], b_f[j, tf, :], carry[2 * j], True)
            hb = tile_scan(a_b[j, tb, :], b_b[j, tb, :], carry[2 * j + 1], False)
            b_f[j, tf, :] = hf
            b_b[j, tb, :] = hb
            out += [hf[7:8, :], hb[0:1, :]]
        return tuple(out)

    init = []
    for j in range(nblk):
        init += [h0_ref[0, 0:1, j * BLK_A:(j + 1) * BLK_A], h0_ref[0, 1:2, j * BLK_A:(j + 1) * BLK_A]]
    last = lax.fori_loop(0, nt, step, tuple(init), unroll=2)
    st_ref[0, 0:1, :] = jnp.concatenate(last[0::2], axis=1)
    st_ref[0, 1:2, :] = jnp.concatenate(last[1::2], axis=1)
    for s in range(0, n, rows):
        hsum = jnp.concatenate([b_f[j, s:s + rows, :] + b_b[j, s:s + rows, :] for j in range(nblk)], axis=1)
        out_ref[s:s + rows, :] = (_gelu_tanh(gate_ref[s:s + rows, :]) * hsum).astype(BF16)


def _rglru(z_a, bsz, n, conv_w, wa, wi, ba, bi, lam, h0):
    d_a = conv_w.shape[1]
    cw = 256
    nb = d_a // cw
    hb = cw // BLK_A
    rows = min(n, 256)
    return pl.pallas_call(
        functools.partial(_lru_body, rows=rows), grid=(bsz, nb),
        in_specs=[pl.BlockSpec((n, cw), lambda b, j: (b, j)),
                  pl.BlockSpec((n, cw), lambda b, j: (b, nb + j)),
                  pl.BlockSpec((4, cw), lambda b, j: (0, j)),
                  pl.BlockSpec((2, hb, BLK_A, BLK_A), lambda b, j: (0, j, 0, 0)),
                  pl.BlockSpec((2, hb, BLK_A, BLK_A), lambda b, j: (0, j, 0, 0)),
                  pl.BlockSpec((2, cw), lambda b, j: (0, j)),
                  pl.BlockSpec((2, cw), lambda b, j: (0, j)),
                  pl.BlockSpec((2, cw), lambda b, j: (0, j)),
                  pl.BlockSpec((1, 2, cw), lambda b, j: (b, 0, j))],
        out_specs=[pl.BlockSpec((n, cw), lambda b, j: (b, j)),
                   pl.BlockSpec((1, 2, cw), lambda b, j: (b, 0, j))],
        out_shape=[jax.ShapeDtypeStruct((bsz * n, d_a), BF16),
                   jax.ShapeDtypeStruct((bsz, 2, d_a), F32)],
        scratch_shapes=[pltpu.VMEM((n + 16, cw), F32)] + [pltpu.VMEM((hb, n, BLK_A), F32)] * 4,
        compiler_params=_cparams(2), name="rglru")(z_a, z_a, conv_w, wa, wi, ba, bi, lam, h0)


def _ret_body(*refs, use_rope, has_state, chunk):
    q_ref, k_ref, v_ref, g_ref, dec_ref = refs[:5]
    pos = 5
    if use_rope:
        cos_ref, sin_ref = refs[pos:pos + 2]
        pos += 2
    if has_state:
        s0_ref = refs[pos]
        pos += 1
    o_ref, st_ref, rb_scr = refs[pos:pos + 3]
    n = q_ref.shape[0]
    c_len = chunk
    nc = n // c_len
    head = pl.program_id(1)

    lg = -_softplus(-dec_ref[...])
    lane = lax.broadcasted_iota(I32, lg.shape, 1)
    lgh = jnp.sum(jnp.where(lane == head, lg, 0.0), axis=1, keepdims=True)
    lgf, lgb = lgh[0:1, :], lgh[1:2, :]

    def prep(ref, c, sc):
        x = ref[c * c_len:(c + 1) * c_len, :].astype(F32) * sc
        if use_rope:
            x = (x * cos_ref[c * c_len:(c + 1) * c_len, :]
                 + pltpu.roll(x, DK_B // 2, 1) * sin_ref[c * c_len:(c + 1) * c_len, :])
        return x

    ii = lax.broadcasted_iota(I32, (c_len, 1), 0).astype(F32)
    jj = lax.broadcasted_iota(I32, (1, c_len), 1).astype(F32)
    rel = ii - jj
    dmask = jnp.where(rel > 0.0, jnp.exp(lgf * jnp.maximum(rel, 0.0)),
                      jnp.where(rel < 0.0, jnp.exp(lgb * jnp.maximum(-rel, 0.0)), 2.0))
    qf_dec = jnp.exp(lgf * (ii + 1.0))
    qb_dec = jnp.exp(lgb * (c_len - ii))
    kf_dec = jnp.exp(lgf * (c_len - 1.0 - ii))
    kb_dec = jnp.exp(lgb * ii)
    gf_c = jnp.exp(lgf * c_len)
    gb_c = jnp.exp(lgb * c_len)

    r_b = s0_ref[0, 1, 0] if has_state else jnp.zeros((DK_B, DV_B), F32)
    for c in range(nc - 1, -1, -1):
        rb_scr[c] = r_b
        k = prep(k_ref, c, 1.0)
        v = v_ref[c * c_len:(c + 1) * c_len, :].astype(BF16)
        r_b = r_b * gb_c + _dot_tn((k * kb_dec).astype(BF16), v)
    st_ref[0, 1, 0] = r_b

    r_f = s0_ref[0, 0, 0] if has_state else jnp.zeros((DK_B, DV_B), F32)
    for c in range(nc):
        q = prep(q_ref, c, DK_B ** -0.5)
        k = prep(k_ref, c, 1.0)
        v = v_ref[c * c_len:(c + 1) * c_len, :].astype(BF16)
        s = _dot_nt(q.astype(BF16), k.astype(BF16)) * dmask
        o = _dot(s.astype(BF16), v)
        qd = jnp.concatenate([q * qf_dec, q * qb_dec], axis=1).astype(BF16)
        rcat = jnp.concatenate([r_f, rb_scr[c]], axis=0).astype(BF16)
        o = o + _dot(qd, rcat)
        r_f = r_f * gf_c + _dot_tn((k * kf_dec).astype(BF16), v)
        on = o * lax.rsqrt(jnp.mean(o * o, axis=-1, keepdims=True) + EPS)
        g = g_ref[c * c_len:(c + 1) * c_len, :].astype(F32)
        o_ref[c * c_len:(c + 1) * c_len, :] = (_silu(g) * on).astype(BF16)
    st_ref[0, 0, 0] = r_f


def _retention(z_b, bsz, n, dec, rope, s0):
    nq = H_B
    chunk = min(n, RET_C)
    in_specs = [pl.BlockSpec((n, DK_B), lambda b, h: (b, h)),
                pl.BlockSpec((n, DK_B), lambda b, h: (b, nq + h)),
                pl.BlockSpec((n, DV_B), lambda b, h: (b, nq + h)),
                pl.BlockSpec((n, DV_B), lambda b, h: (b, 2 * nq + h)),
                pl.BlockSpec((2, H_B), lambda b, h: (0, 0))]
    args = [z_b, z_b, z_b, z_b, dec]
    if rope is not None:
        in_specs += [pl.BlockSpec((n, DK_B), lambda b, h: (0, 0))] * 2
        args += list(rope)
    if s0 is not None:
        in_specs.append(pl.BlockSpec((1, 2, 1, DK_B, DV_B), lambda b, h: (b, 0, h, 0, 0)))
        args.append(s0)
    return pl.pallas_call(
        functools.partial(_ret_body, use_rope=rope is not None, has_state=s0 is not None, chunk=chunk),
        grid=(bsz, H_B), in_specs=in_specs,
        out_specs=[pl.BlockSpec((n, DV_B), lambda b, h: (b, h)),
                   pl.BlockSpec((1, 2, 1, DK_B, DV_B), lambda b, h: (b, 0, h, 0, 0))],
        out_shape=[jax.ShapeDtypeStruct((bsz * n, H_B * DV_B), BF16),
                   jax.ShapeDtypeStruct((bsz, 2, H_B, DK_B, DV_B), F32)],
        scratch_shapes=[pltpu.VMEM((n // chunk, DK_B, DV_B), F32)],
        compiler_params=_cparams(2), name="retention")(*args)


def _rope_tables(n):
    rows = n // GRID_W
    n_freq = DK_B // 4
    row = jnp.repeat(jnp.arange(rows, dtype=F32), GRID_W)
    col = (jnp.arange(n) % GRID_W).astype(F32)
    inv = ROPE_BASE ** (-jnp.arange(n_freq, dtype=F32) / n_freq)
    ang = jnp.concatenate([row[:, None] * inv, col[:, None] * inv], axis=-1)
    cos, sin = jnp.cos(ang), jnp.sin(ang)
    return jnp.concatenate([cos, cos], axis=-1), jnp.concatenate([-sin, sin], axis=-1)


def _gdn_body(*refs, has_state, rows):
    (zq_ref, zk_ref, zv_ref, zg_ref, cq_ref, ck_ref, cv_ref, ab_ref, gr_ref,
     a8_ref, dt8_ref, ar_ref, dtr_ref, ng_ref) = refs[:14]
    pos = 14
    if has_state:
        s0_ref = refs[pos]
        pos += 1
    (o_ref, st_ref, xpad, xpad_v, q_scr, k_scr, v_scr, o_scr, bg_scr, grow_scr, s_scr,
     m_scr, blk_scr, u_scr, w_scr, at_scr, qd_scr, kd_scr, eg_scr) = refs[pos:]
    n = zq_ref.shape[0]
    c_len = GDN_C
    c4 = 4 * c_len
    nc = n // c_len
    lg = c_len.bit_length() - 1

    _fill_padded(xpad, zq_ref)
    for s in range(0, n, rows):
        q = _silu(_conv_rows(xpad, cq_ref[...], s, rows))
        q = q * lax.rsqrt(jnp.sum(q * q, axis=-1, keepdims=True) + EPS) * (DK_C ** -0.5)
        q_scr[s:s + rows, :] = q.astype(BF16)
    _fill_padded(xpad, zk_ref)
    for s in range(0, n, rows):
        k = _silu(_conv_rows(xpad, ck_ref[...], s, rows))
        k = k * lax.rsqrt(jnp.sum(k * k, axis=-1, keepdims=True) + EPS)
        k_scr[s:s + rows, :] = k.astype(BF16)
    _fill_padded(xpad_v, zv_ref)
    for s in range(0, n, rows):
        v_scr[s:s + rows, :] = _silu(_conv_rows(xpad_v, cv_ref[...], s, rows))

    ab = ab_ref[0, 0]
    gcols = -jnp.exp(a8_ref[0]) * _softplus(ab + dt8_ref[0])
    w_is_beta = ((lax.broadcasted_iota(I32, ab.shape, 1) >> 1) & 1) == 0
    bg_scr[...] = jnp.where(w_is_beta, jax.nn.sigmoid(ab), gcols)
    grow_scr[...] = -jnp.exp(ar_ref[0]) * _softplus(gr_ref[0, 0] + dtr_ref[0])
    o_scr[...] = jnp.zeros(o_scr.shape, F32)
    for d in range(2):
        if has_state:
            s_scr[d] = jnp.concatenate([s0_ref[0, d, 0], s0_ref[0, d, 1]], axis=0)
        else:
            s_scr[d] = jnp.zeros((2 * DK_C, DV_C), F32)

    ii = lax.broadcasted_iota(I32, (c_len, c4), 0)
    ll = lax.broadcasted_iota(I32, (c_len, c4), 1)
    jj = ll & (c_len - 1)
    lane_stream = ll >> lg
    fwd = lane_stream < 2
    one = lambda m: jnp.where(m, 1.0, 0.0)
    m_scr[0] = jnp.where(fwd, one(jj <= ii), one(jj >= ii))
    m_scr[1] = jnp.where(fwd, one(ii <= jj), one(ii >= jj))
    m_scr[2] = one((ii >> 3) == (jj >> 3))
    n_merge = lg - 3
    for t in range(n_merge):
        sh = 3 + t
        m_scr[3 + t] = jnp.where((ii >> (sh + 1)) == (jj >> (sh + 1)), one((ii >> sh) != (jj >> sh)), 0.0)
    eye = one(ii == jj)
    blk_scr[...] = one((lax.broadcasted_iota(I32, (c4, c4), 0) >> lg)
                       == (lax.broadcasted_iota(I32, (c4, c4), 1) >> lg)).astype(BF16)
    ci = lax.broadcasted_iota(I32, (c_len, c_len), 0)
    cj = lax.broadcasted_iota(I32, (c_len, c_len), 1)
    tri = (one(cj <= ci), one(cj >= ci))
    is_head0 = ((lax.broadcasted_iota(I32, (c4, DK_C), 0) >> lg) & 1) == 0

    def by_head(x):
        return jnp.concatenate([jnp.where(is_head0, x, 0.0), jnp.where(is_head0, 0.0, x)], axis=1).astype(BF16)

    def by_stream(cols):
        return jnp.where(fwd, jnp.where(lane_stream == 0, cols[0], cols[1]),
                         jnp.where(lane_stream == 2, cols[2], cols[3]))

    def block_diag(x):
        return jnp.concatenate([x.astype(BF16)] * 4, axis=0) * blk_scr[...]

    def tri_inverse(a_mats):
        a_d = [a * m_scr[2] for a in a_mats]
        a2 = [_dot(a.astype(BF16), block_diag(a)) for a in a_d]
        ps = [eye - a for a in a_d]
        both = [_dot(jnp.concatenate([p, x], axis=0).astype(BF16), block_diag(x)) for p, x in zip(ps, a2)]
        ps = [p + b[:c_len] for p, b in zip(ps, both)]
        ps = [p + _dot(p.astype(BF16), block_diag(b[c_len:])) for p, b in zip(ps, both)]
        for t in range(n_merge):
            zs = [_dot(p.astype(BF16), block_diag(a * m_scr[3 + t])) for p, a in zip(ps, a_mats)]
            ps = [p - _dot(z.astype(BF16), block_diag(p)) for p, z in zip(ps, zs)]
        return ps

    group = min(GDN_GROUP, nc)

    def prep_group(gi, carry):
        cs = [gi * group + t for t in range(group)]
        rows_c = [pl.ds(pl.multiple_of(c * c_len, c_len), c_len) for c in cs]
        out_rows = [pl.ds(pl.multiple_of(c * c4, c4), c4) for c in cs]
        kc = [k_scr[r, :] for r in rows_c]
        qc = [q_scr[r, :] for r in rows_c]
        kc4 = [jnp.concatenate([k] * 4, axis=0) for k in kc]
        qc4 = [jnp.concatenate([q] * 4, axis=0) for q in qc]
        gram = [_dot_nt(jnp.concatenate([k, q], axis=0), k4) for k, q, k4 in zip(kc, qc, kc4)]
        kk = [g[:c_len] for g in gram]
        qk = [g[c_len:] for g in gram]
        incl = m_scr[0]
        beta, gc_col, g_last, decay, a_mats = [], [], [], [], []
        for t in range(group):
            bg = bg_scr[rows_c[t], :]
            beta_s = [bg[:, 0:1], bg[:, 1:2], bg[:, 4:5], bg[:, 5:6]]
            gcol_s = [bg[:, 2:3], bg[:, 3:4], bg[:, 6:7], bg[:, 7:8]]
            grow = grow_scr[pl.ds(cs[t], 1), :]
            gc_s = [jnp.sum(tri[s // 2] * grow[:, s * c_len:(s + 1) * c_len], axis=1, keepdims=True)
                    for s in range(4)]
            gc_row = jnp.sum(m_scr[1] * by_stream(gcol_s), axis=0, keepdims=True)
            decay.append(jnp.exp((by_stream(gc_s) - gc_row) * incl) * incl)
            a_mats.append((by_stream(beta_s) * kk[t] * decay[t]) * (incl - eye))
            gl_s = [gc_s[s][c_len - 1:c_len, :] if s < 2 else gc_s[s][0:1, :] for s in range(4)]
            beta.append(jnp.concatenate(beta_s, axis=0))
            gc_col.append(jnp.concatenate(gc_s, axis=0))
            g_last.append(jnp.concatenate([jnp.broadcast_to(g, (c_len, 1)) for g in gl_s], axis=0))
        t_inv = tri_inverse(a_mats)
        blk2 = blk_scr[0:2 * c_len, 0:2 * c_len]
        for t in range(group):
            kf = kc4[t].astype(F32)
            qf = qc4[t].astype(F32)
            egc = jnp.exp(gc_col[t])
            vc = v_scr[rows_c[t], :]
            v4 = jnp.concatenate([vc[:, :DV_C], vc[:, DV_C:], vc[:, :DV_C], vc[:, DV_C:]], axis=0)
            kb = kf * (beta[t] * egc)
            rhs = jnp.concatenate([v4 * beta[t], kb], axis=1).astype(BF16)
            uw = _dot(block_diag(t_inv[t]), rhs)
            u_scr[out_rows[t], :] = uw[:, :DV_C]
            w = uw[:, DV_C:]
            w_scr[out_rows[t], :] = by_head(w)
            attn = (qk[t] * decay[t]).astype(BF16)
            at_scr[out_rows[t], :] = jnp.concatenate(
                [jnp.concatenate([attn[:, :2 * c_len]] * 2, axis=0) * blk2,
                 jnp.concatenate([attn[:, 2 * c_len:]] * 2, axis=0) * blk2], axis=0)
            qe = qf * egc
            qd_scr[out_rows[t], :] = by_head(qe)
            ke = kf * jnp.exp(g_last[t] - gc_col[t])
            kd_scr[out_rows[t], :] = by_head(ke)
            egl = jnp.exp(g_last[t])
            for s in range(4):
                eg_scr[pl.ds(cs[t] * 4 + s, 1), :] = jnp.broadcast_to(egl[s * c_len:s * c_len + 1, :], (1, DV_C))
        return carry

    lax.fori_loop(0, nc // group, prep_group, 0)

    def scan_chunk(c, carry):
        ccs = [c, nc - 1 - c]
        rows_d = [pl.ds(pl.multiple_of(ccs[d] * c4 + d * 2 * c_len, 2 * c_len), 2 * c_len) for d in range(2)]
        s_cat = [s_scr[d] for d in range(2)]
        sb = [s.astype(BF16) for s in s_cat]
        ws = [_dot(w_scr[rows_d[d], :], sb[d]) for d in range(2)]
        qs = [_dot(qd_scr[rows_d[d], :], sb[d]) for d in range(2)]
        vnb = [(u_scr[rows_d[d], :] - ws[d]).astype(BF16) for d in range(2)]
        kv = [_dot_tn(kd_scr[rows_d[d], :], vnb[d]) for d in range(2)]
        av = [_dot(at_scr[rows_d[d], :], vnb[d]) for d in range(2)]
        for d in range(2):
            e0 = jnp.broadcast_to(eg_scr[pl.ds(ccs[d] * 4 + 2 * d, 1), :], (DK_C, DV_C))
            e1 = jnp.broadcast_to(eg_scr[pl.ds(ccs[d] * 4 + 2 * d + 1, 1), :], (DK_C, DV_C))
            s_scr[d] = s_cat[d] * jnp.concatenate([e0, e1], axis=0) + kv[d]
            o = qs[d] + av[d]
            tok = pl.ds(pl.multiple_of(ccs[d] * c_len, c_len), c_len)
            o_scr[tok, 0:DV_C] += o[:c_len]
            o_scr[tok, DV_C:2 * DV_C] += o[c_len:]
        return carry

    lax.fori_loop(0, nc, scan_chunk, 0)

    for d in range(2):
        for r in range(2):
            st_ref[0, d, r] = s_scr[d, r * DK_C:(r + 1) * DK_C, :]
    ng = ng_ref[...]
    for s in range(0, n, rows):
        zg = zg_ref[s:s + rows, :].astype(F32)
        for r in range(2):
            o = o_scr[s:s + rows, r * DV_C:(r + 1) * DV_C]
            on = o * lax.rsqrt(jnp.mean(o * o, axis=-1, keepdims=True) + EPS) * ng
            o_ref[s:s + rows, r * DV_C:(r + 1) * DV_C] = (
                on * _silu(zg[:, r * DV_C:(r + 1) * DV_C])).astype(BF16)


def _gdn(z1, ab, bsz, n, conv_c, a_log, dt_bias, norm_g, s0):
    nc = n // GDN_C
    rows = min(n, 256)
    nqb = H_QK
    ab6 = ab.reshape(bsz, n, 2, 2, H_QK, 2)
    ab8 = ab6.transpose(0, 4, 1, 2, 3, 5).reshape(bsz, H_QK, n, 8)
    g_rows = ab6[:, :, :, 1].reshape(bsz, nc, GDN_C, 2, H_QK, 2).transpose(0, 4, 1, 3, 5, 2)
    g_rows = g_rows.reshape(bsz, H_QK, nc, 4 * GDN_C)

    def per_pair(p):
        p3 = p.astype(F32).reshape(2, H_QK, 2).transpose(1, 0, 2)
        return jnp.broadcast_to(p3[:, :, None, :], (H_QK, 2, 2, 2)).reshape(H_QK, 8)

    def per_stream(p):
        p3 = p.astype(F32).reshape(2, H_QK, 2).transpose(1, 0, 2).reshape(H_QK, 4)
        return jnp.repeat(p3, GDN_C, axis=1).reshape(H_QK, 1, 4 * GDN_C)

    a8, dt8 = per_pair(a_log), per_pair(dt_bias)
    in_specs = [pl.BlockSpec((n, DK_C), lambda b, j: (b, j)),
                pl.BlockSpec((n, DK_C), lambda b, j: (b, nqb + j)),
                pl.BlockSpec((n, 2 * DV_C), lambda b, j: (b, nqb + j)),
                pl.BlockSpec((n, 2 * DV_C), lambda b, j: (b, 2 * nqb + j)),
                pl.BlockSpec((4, DK_C), lambda b, j: (0, j)),
                pl.BlockSpec((4, DK_C), lambda b, j: (0, nqb + j)),
                pl.BlockSpec((4, 2 * DV_C), lambda b, j: (0, nqb + j)),
                pl.BlockSpec((1, 1, n, 8), lambda b, j: (b, j, 0, 0)),
                pl.BlockSpec((1, 1, nc, 4 * GDN_C), lambda b, j: (b, j, 0, 0)),
                pl.BlockSpec((1, 1, 8), lambda b, j: (j, 0, 0)),
                pl.BlockSpec((1, 1, 8), lambda b, j: (j, 0, 0)),
                pl.BlockSpec((1, 1, 4 * GDN_C), lambda b, j: (j, 0, 0)),
                pl.BlockSpec((1, 1, 4 * GDN_C), lambda b, j: (j, 0, 0)),
                pl.BlockSpec((1, DV_C), lambda b, j: (0, 0))]
    args = [z1, z1, z1, z1, conv_c, conv_c, conv_c, ab8, g_rows,
            a8.reshape(H_QK, 1, 8), dt8.reshape(H_QK, 1, 8),
            per_stream(a_log), per_stream(dt_bias), norm_g.reshape(1, DV_C)]
    if s0 is not None:
        in_specs.append(pl.BlockSpec((1, 2, 2, DK_C, DV_C), lambda b, j: (b, 0, j, 0, 0)))
        args.append(s0)
    return pl.pallas_call(
        functools.partial(_gdn_body, has_state=s0 is not None, rows=rows),
        grid=(bsz, H_QK), in_specs=in_specs,
        out_specs=[pl.BlockSpec((n, 2 * DV_C), lambda b, j: (b, j)),
                   pl.BlockSpec((1, 2, 2, DK_C, DV_C), lambda b, j: (b, 0, j, 0, 0))],
        out_shape=[jax.ShapeDtypeStruct((bsz * n, H_V * DV_C), BF16),
                   jax.ShapeDtypeStruct((bsz, 2, H_V, DK_C, DV_C), F32)],
        scratch_shapes=[pltpu.VMEM((n + 16, DK_C), F32), pltpu.VMEM((n + 16, 2 * DV_C), F32),
                        pltpu.VMEM((n, DK_C), BF16), pltpu.VMEM((n, DK_C), BF16),
                        pltpu.VMEM((n, 2 * DV_C), F32), pltpu.VMEM((n, 2 * DV_C), F32),
                        pltpu.VMEM((n, 8), F32), pltpu.VMEM((nc, 4 * GDN_C), F32),
                        pltpu.VMEM((2, 2 * DK_C, DV_C), F32),
                        pltpu.VMEM((GDN_C.bit_length() - 1, GDN_C, 4 * GDN_C), F32),
                        pltpu.VMEM((4 * GDN_C, 4 * GDN_C), BF16),
                        pltpu.VMEM((4 * n, DV_C), F32), pltpu.VMEM((4 * n, 2 * DK_C), BF16),
                        pltpu.VMEM((4 * n, 2 * GDN_C), BF16), pltpu.VMEM((4 * n, 2 * DK_C), BF16),
                        pltpu.VMEM((4 * n, 2 * DK_C), BF16), pltpu.VMEM((4 * nc, DV_C), F32)],
        compiler_params=_cparams(2), name="gated_deltanet")(*args)


def _router_body(x_ref, m_ref, g_ref, rwt_ref, h_ref, aff_ref):
    h = _adaln(x_ref[...], g_ref[...], m_ref[3:4, :], m_ref[4:5, :])
    h_ref[...] = h.astype(BF16)
    logits = _dot_nt(rwt_ref[...].astype(BF16), h.astype(BF16))
    ex = jnp.exp(logits - jnp.max(logits, axis=0, keepdims=True))
    aff_ref[0] = ex / jnp.sum(ex, axis=0, keepdims=True)


def _router(x, mod_l, row_fn, g, rw_t, bsz, n, tt):
    t, d = x.shape
    e = rw_t.shape[0]
    nt = n // tt
    return pl.pallas_call(
        _router_body, grid=(bsz, nt),
        in_specs=[pl.BlockSpec((tt, d), lambda b, i: (b * nt + i, 0)),
                  pl.BlockSpec((None, 6, d), lambda b, i: (row_fn(b), 0, 0)),
                  pl.BlockSpec((1, d), lambda b, i: (0, 0)),
                  pl.BlockSpec((e, d), lambda b, i: (0, 0))],
        out_specs=[pl.BlockSpec((tt, d), lambda b, i: (b * nt + i, 0)),
                   pl.BlockSpec((1, e, tt), lambda b, i: (b, 0, i))],
        out_shape=[jax.ShapeDtypeStruct((t, d), BF16), jax.ShapeDtypeStruct((bsz, e, n), F32)],
        compiler_params=_cparams(2), name="adaln_router")(x, mod_l, g.reshape(1, d), rw_t)


def _cumsum_lanes(m):
    r, n = m.shape
    blk = min(n, 256)
    tri = jnp.where(lax.broadcasted_iota(I32, (blk, blk), 0) <= lax.broadcasted_iota(I32, (blk, blk), 1),
                    1.0, 0.0).astype(BF16)
    outs = []
    run = jnp.zeros((r, 1), F32)
    for c in range(n // blk):
        loc = _dot(m[:, c * blk:(c + 1) * blk], tri) + run
        outs.append(loc)
        run = loc[:, blk - 1:blk]
    return outs[0] if len(outs) == 1 else jnp.concatenate(outs, axis=1)


def _topk_body(aff_ref, pos_ref, *, cap):
    bb, e, n = aff_ref.shape
    rows = bb * e
    bits = pltpu.bitcast(aff_ref[...].reshape(rows, n), I32)
    capf = float(cap)
    prefix = jnp.zeros((rows, 1), I32)
    for bit in range(30, -1, -1):
        cand = prefix | (1 << bit)
        cnt = jnp.sum(jnp.where(bits >= cand, 1.0, 0.0), axis=1, keepdims=True)
        prefix = jnp.where(cnt >= capf, cand, prefix)
    gt = bits > prefix
    eq = bits == prefix
    need = capf - jnp.sum(jnp.where(gt, 1.0, 0.0), axis=1, keepdims=True)
    eq_rank = _cumsum_lanes(jnp.where(eq, 1.0, 0.0).astype(BF16))
    sel = jnp.where(gt, 1.0, jnp.where(eq, jnp.where(eq_rank <= need, 1.0, 0.0), 0.0))
    slot = _cumsum_lanes(sel.astype(BF16)) - 1.0
    pos_ref[...] = jnp.where(sel > 0.5, slot, -1.0).astype(I32).reshape(bb, e, n)


def _topk(aff, cap):
    bsz, e, n = aff.shape
    return pl.pallas_call(
        functools.partial(_topk_body, cap=cap), grid=(1,),
        in_specs=[pl.BlockSpec((bsz, e, n), lambda i: (0, 0, 0))],
        out_specs=pl.BlockSpec((bsz, e, n), lambda i: (0, 0, 0)),
        out_shape=jax.ShapeDtypeStruct((bsz, e, n), I32),
        compiler_params=_cparams(1), name="expert_choice_topk")(aff)


def _gather_body(h_ref, pos_ref, aff_ref, xg_ref, gate_ref):
    eg, cap, _ = xg_ref.shape
    n = h_ref.shape[0]
    slot = lax.broadcasted_iota(I32, (cap, n), 0)
    for e in range(eg):
        hit = slot == pos_ref[0, e]
        onehot = jnp.where(hit, 1.0, 0.0).astype(BF16)
        xg_ref[e] = _dot(onehot, h_ref[...]).astype(BF16)
        gate_ref[e] = jnp.sum(jnp.where(hit, aff_ref[0, e], 0.0), axis=1, keepdims=True)


def _gather(h, pos4, aff4, bsz, n, cap, eg):
    d = h.shape[1]
    e = pos4.shape[1]
    return pl.pallas_call(
        _gather_body, grid=(bsz, e // eg),
        in_specs=[pl.BlockSpec((n, d), lambda b, g: (b, 0)),
                  pl.BlockSpec((1, eg, 1, n), lambda b, g: (b, g, 0, 0)),
                  pl.BlockSpec((1, eg, 1, n), lambda b, g: (b, g, 0, 0))],
        out_specs=[pl.BlockSpec((eg, cap, d), lambda b, g: (g, b, 0)),
                   pl.BlockSpec((eg, cap, 1), lambda b, g: (g, b, 0))],
        out_shape=[jax.ShapeDtypeStruct((e, bsz * cap, d), BF16),
                   jax.ShapeDtypeStruct((e, bsz * cap, 1), F32)],
        compiler_params=_cparams(2), name="moe_gather")(h, pos4, aff4)


def _ffn_body(x_ref, gate_ref, w1_ref, w3_ref, w2_ref, o_ref, hid, *, n_up):
    s = pl.program_id(2)

    @pl.when(s < n_up)
    def _():
        x = x_ref[0]
        a = _dot(x, w1_ref[0].astype(BF16))
        b = _dot(x, w3_ref[0].astype(BF16))
        hid[s] = (_silu(a) * b).astype(BF16)

    @pl.when(s >= n_up)
    def _():
        h = jnp.concatenate([hid[k] for k in range(n_up)], axis=1)
        o_ref[0] = (_dot(h, w2_ref[0].astype(BF16)) * gate_ref[0]).astype(BF16)


def _ffn(xg, gate, w1, w3, w2, layer, tm, tf):
    e, m, d = xg.shape
    ff = w1.shape[3]
    n_up = ff // tf
    up = lambda s: jnp.minimum(s, n_up - 1)
    down = lambda s: jnp.maximum(s - n_up, 0)
    return pl.pallas_call(
        functools.partial(_ffn_body, n_up=n_up), grid=(e, m // tm, n_up + d // tf),
        in_specs=[pl.BlockSpec((1, tm, d), lambda x, i, s: (x, i, 0)),
                  pl.BlockSpec((1, tm, 1), lambda x, i, s: (x, i, 0)),
                  pl.BlockSpec((None, 1, d, tf), lambda x, i, s: (layer, x, 0, up(s))),
                  pl.BlockSpec((None, 1, d, tf), lambda x, i, s: (layer, x, 0, up(s))),
                  pl.BlockSpec((None, 1, ff, tf), lambda x, i, s: (layer, x, 0, down(s)))],
        out_specs=pl.BlockSpec((1, tm, tf), lambda x, i, s: (x, i, down(s))),
        out_shape=jax.ShapeDtypeStruct((e, m, d), BF16),
        scratch_shapes=[pltpu.VMEM((n_up, tm, tf), BF16)],
        compiler_params=_cparams(3), name="expert_ffn")(xg, gate, w1, w3, w2)


def _combine_body(pos_ref, y_ref, x_ref, m_ref, fg_ref, o_ref, acc, *, final):
    e = pl.program_id(2)
    eg, cap, d = y_ref.shape
    tt = x_ref.shape[0]
    rows8 = [pos_ref[0, g] for g in range(eg)] + [jnp.zeros((1, tt), I32)] * ((-eg) % 8)
    pos_rows = jnp.concatenate(rows8, axis=0).astype(F32).astype(BF16)
    eye = jnp.where(lax.broadcasted_iota(I32, (tt, tt), 0) == lax.broadcasted_iota(I32, (tt, tt), 1),
                    1.0, 0.0).astype(BF16)
    pos_cols = _dot_nt(eye, pos_rows)
    slot = lax.broadcasted_iota(I32, (tt, cap), 1).astype(F32)
    onehot = jnp.concatenate([jnp.where(slot == pos_cols[:, g:g + 1], 1.0, 0.0).astype(BF16)
                              for g in range(eg)], axis=1)
    part = _dot(onehot, y_ref[...].reshape(eg * cap, d))

    @pl.when(e == 0)
    def _():
        acc[...] = part

    @pl.when(e > 0)
    def _():
        acc[...] += part

    @pl.when(e == pl.num_programs(2) - 1)
    def _():
        xn = x_ref[...] + m_ref[5:6, :] * acc[...]
        if final:
            xn = xn * lax.rsqrt(jnp.mean(xn * xn, axis=-1, keepdims=True) + EPS) * fg_ref[...]
        o_ref[...] = xn


def _combine(pos4, y, x, mod_l, row_fn, final_g, bsz, n, cap, tt, eg, final):
    t, d = x.shape
    e = y.shape[0]
    nt = n // tt
    return pl.pallas_call(
        functools.partial(_combine_body, final=final), grid=(bsz, nt, e // eg),
        in_specs=[pl.BlockSpec((1, eg, 1, tt), lambda b, i, x_: (b, x_, 0, i)),
                  pl.BlockSpec((eg, cap, d), lambda b, i, x_: (x_, b, 0)),
                  pl.BlockSpec((tt, d), lambda b, i, x_: (b * nt + i, 0)),
                  pl.BlockSpec((None, 6, d), lambda b, i, x_: (row_fn(b), 0, 0)),
                  pl.BlockSpec((1, d), lambda b, i, x_: (0, 0))],
        out_specs=pl.BlockSpec((tt, d), lambda b, i, x_: (b * nt + i, 0)),
        out_shape=jax.ShapeDtypeStruct((t, d), F32),
        scratch_shapes=[pltpu.VMEM((tt, d), F32)],
        compiler_params=_cparams(3), name="moe_combine")(pos4, y, x, mod_l, final_g.reshape(1, d))


def _ec_moe(x, mod_l, row_fn, norm_g, rw_t, w1, w3, w2, layer, final_g, bsz, n, final):
    e = rw_t.shape[0]
    cap = EC_FACTOR * n // e
    tt = min(n, 512)
    h, aff = _router(x, mod_l, row_fn, norm_g, rw_t, bsz, n, tt)
    pos = _topk(aff, cap)
    pos4 = pos.reshape(bsz, e, 1, n)
    eg = e if cap * e <= 1024 else 1
    xg, gate = _gather(h, pos4, aff.reshape(bsz, e, 1, n), bsz, n, cap, eg)
    m = bsz * cap
    tm = m if m <= 1024 else 1024
    y = _ffn(xg, gate, w1, w3, w2, layer, tm, 512)
    eg_c = max(1, min(e, 1024 // cap))
    return _combine(pos4, y, x, mod_l, row_fn, final_g, bsz, n, cap, tt, eg_c, final)


def _run_group(x3, mod, row_of_request, states, p, with_rope):
    bsz, n, d = x3.shape
    x = x3.reshape(bsz * n, d)
    tm = min(n, 1024) if row_of_request is not None else min(bsz * n, 1024)
    if row_of_request is None:
        tile_row = lambda i: 0
        req_row = lambda b: 0
    else:
        tiles_per_req = n // tm
        tile_row = lambda i: row_of_request + i // tiles_per_req
        req_row = lambda b: row_of_request + b
    s_lru, s_ret, s_gdn = states
    outs = {}
    depth = p["mod"].shape[0]
    for l in range(depth):
        mod_l = mod[l]
        last = l == depth - 1
        if l % 2 == 0:
            e = l // 2
            w_in = p["w_in0"][e]
            d_a = p["conv_a"].shape[-1]
            z_a = _inproj(x, mod_l, tile_row, p["norm1_g"][l], w_in, 0, 2 * d_a, F32, tm, 512, 0, 1)
            z_b = _inproj(x, mod_l, tile_row, p["norm1_g"][l], w_in, 2 * d_a, w_in.shape[1] - 2 * d_a,
                          ACT_DTYPE, tm, 512, 0, 1)
            h0 = jnp.zeros((bsz, 2, d_a), F32) if s_lru is None else s_lru[:, e]
            out_a, st_a = _rglru(z_a, bsz, n, p["conv_a"][e], p["lru_wa"][e], p["lru_wi"][e],
                                 p["lru_ba"][e], p["lru_bi"][e], p["lru_lam"][e], h0)
            out_b, st_b = _retention(z_b, bsz, n, p["ret_decay"][e], _rope_tables(n) if with_rope else None,
                                     None if s_ret is None else s_ret[:, e])
            outs.setdefault("lru", []).append(st_a)
            outs.setdefault("ret", []).append(st_b)
            x = _outproj([out_a, out_b], p["w_out0"][e], x, mod_l, tile_row, tm, 512, 2)
        else:
            o = l // 2
            w_in = p["w_in1"][o]
            nz = H_QK * DK_C * 2 + 2 * H_V * DV_C
            z1 = _inproj(x, mod_l, tile_row, p["norm1_g"][l], w_in, 0, nz, ACT_DTYPE, tm, 512, 0, 1)
            ab = _inproj(x, mod_l, tile_row, p["norm1_g"][l], w_in, nz, 4 * H_V, F32, tm, 4 * H_V, 0, 1)
            out_c, st_c = _gdn(z1, ab, bsz, n, p["conv_c"][o], p["gdn_a_log"][o], p["gdn_dt_bias"][o],
                               p["gdn_norm_g"][o], None if s_gdn is None else s_gdn[:, o])
            outs.setdefault("gdn", []).append(st_c)
            x = _outproj([out_c], p["w_out1"][o], x, mod_l, tile_row, tm, 512, 2)
        x = _ec_moe(x, mod_l, req_row, p["norm2_g"][l], p["router_t"][l], p["exp_w1"], p["exp_w3"],
                    p["exp_w2"], l, p["final_g"], bsz, n, last)
    return x.reshape(bsz, n, d), outs


def kernel(x_prompt, x_sample, state_rglru, state_ret, state_gdn, c, c_ctx, mod_w, mod_b, norm1_g, norm2_g, w_in0, conv_a, lru_wa, lru_ba, lru_wi, lru_bi, lru_lam, ret_decay, w_out0, w_in1, conv_c, gdn_a_log, gdn_dt_bias, gdn_norm_g, w_out1, router_w, exp_w1, exp_w3, exp_w2, final_g):
    pad = (-(c.shape[0] + 1)) % 8
    cond = jnp.concatenate([c_ctx[None, :], c, jnp.zeros((pad, c.shape[1]), F32)], axis=0)
    mod = _modulation(cond, mod_w, mod_b)
    p = dict(mod=mod, norm1_g=norm1_g, norm2_g=norm2_g,
             w_in0=w_in0.astype(BF16), conv_a=conv_a, lru_wa=lru_wa, lru_ba=lru_ba, lru_wi=lru_wi,
             lru_bi=lru_bi, lru_lam=lru_lam, ret_decay=ret_decay, w_out0=w_out0.astype(BF16),
             w_in1=w_in1.astype(BF16), conv_c=conv_c, gdn_a_log=gdn_a_log, gdn_dt_bias=gdn_dt_bias,
             gdn_norm_g=gdn_norm_g, w_out1=w_out1.astype(BF16), router_t=jnp.swapaxes(router_w, 1, 2),
             exp_w1=exp_w1, exp_w3=exp_w3, exp_w2=exp_w2, final_g=final_g)
    y_prompt, st = _run_group(x_prompt, mod, None, (None, None, None), p, False)
    y_sample, _ = _run_group(x_sample, mod, 1, (state_rglru, state_ret, state_gdn), p, True)
    return (y_prompt, y_sample, jnp.stack(st["lru"], axis=1), jnp.stack(st["ret"], axis=1),
            jnp.stack(st["gdn"], axis=1))
```

```python
import functools

import jax
import jax.numpy as jnp
from jax import lax
from jax.experimental import pallas as pl
from jax.experimental.pallas import tpu as pltpu

F32 = jnp.float32
BF16 = jnp.bfloat16
I32 = jnp.int32
ACT_DTYPE = BF16
EPS = 1e-6
GRID_W = 64
LRU_C = 8.0
ROPE_BASE = 10000.0
H_A, BLK_A = 16, 128
H_B, DK_B, DV_B = 8, 128, 256
H_QK, H_V, DK_C, DV_C = 16, 32, 128, 128
GDN_C = 64
GDN_GROUP = 4
RET_C = 256
N_EXPERTS = 16
EC_FACTOR = 2
V7X_VMEM_LIMIT = 56 * 1024 * 1024


def _cparams(n_axes):
    return pltpu.CompilerParams(dimension_semantics=("arbitrary",) * n_axes,
                                vmem_limit_bytes=V7X_VMEM_LIMIT)


def _silu(x):
    return x * jax.nn.sigmoid(x)


def _softplus(x):
    return jnp.maximum(x, 0.0) + jnp.log1p(jnp.exp(-jnp.abs(x)))


def _gelu_tanh(x):
    return 0.5 * x * (1.0 + jnp.tanh(0.7978845608028654 * (x + 0.044715 * (x * x * x))))


def _adaln(x, g, shift, scale):
    r = lax.rsqrt(jnp.mean(x * x, axis=-1, keepdims=True) + EPS)
    return x * r * g * (1.0 + scale) + shift


def _dot(a, b):
    return jnp.dot(a, b, preferred_element_type=F32)


def _dot_nt(a, b):
    return lax.dot_general(a, b, (((1,), (1,)), ((), ())), preferred_element_type=F32)


def _dot_tn(a, b):
    return lax.dot_general(a, b, (((0,), (0,)), ((), ())), preferred_element_type=F32)


def _mod_body(c_ref, w_ref, b_ref, o_ref):
    s = _silu(c_ref[...]).astype(BF16)
    o_ref[0] = _dot(s, w_ref[0].astype(BF16)) + b_ref[0]


def _modulation(cond, mod_w, mod_b):
    n_l, d, d6 = mod_w.shape
    r = cond.shape[0]
    tn = 1024
    out = pl.pallas_call(
        _mod_body, grid=(n_l, d6 // tn),
        in_specs=[pl.BlockSpec((r, d), lambda l, j: (0, 0)),
                  pl.BlockSpec((1, d, tn), lambda l, j: (l, 0, j)),
                  pl.BlockSpec((1, 1, tn), lambda l, j: (l, 0, j))],
        out_specs=pl.BlockSpec((1, r, tn), lambda l, j: (l, 0, j)),
        out_shape=jax.ShapeDtypeStruct((n_l, r, d6), F32),
        compiler_params=_cparams(2), name="modulation")(cond, mod_w, mod_b.reshape(n_l, 1, d6))
    return out.reshape(n_l, r, 6, d)


def _inproj_body(x_ref, m_ref, g_ref, w_ref, o_ref, h_scr, *, shift_idx, scale_idx):
    @pl.when(pl.program_id(1) == 0)
    def _():
        h = _adaln(x_ref[...], g_ref[...], m_ref[shift_idx:shift_idx + 1, :], m_ref[scale_idx:scale_idx + 1, :])
        h_scr[...] = h.astype(BF16)
    o_ref[...] = _dot(h_scr[...], w_ref[...]).astype(o_ref.dtype)


def _inproj(x, mod_l, row_fn, g, w, col0, ncols, out_dtype, tm, tn, shift_idx, scale_idx):
    t, d = x.shape
    c0 = col0 // tn
    return pl.pallas_call(
        functools.partial(_inproj_body, shift_idx=shift_idx, scale_idx=scale_idx),
        grid=(t // tm, ncols // tn),
        in_specs=[pl.BlockSpec((tm, d), lambda i, j: (i, 0)),
                  pl.BlockSpec((None, 6, d), lambda i, j: (row_fn(i), 0, 0)),
                  pl.BlockSpec((1, d), lambda i, j: (0, 0)),
                  pl.BlockSpec((d, tn), lambda i, j: (0, c0 + j))],
        out_specs=pl.BlockSpec((tm, tn), lambda i, j: (i, j)),
        out_shape=jax.ShapeDtypeStruct((t, ncols), out_dtype),
        scratch_shapes=[pltpu.VMEM((tm, d), BF16)],
        compiler_params=_cparams(2), name="adaln_inproj")(x, mod_l, g.reshape(1, d), w)


def _outproj_body(*refs, n_in, gate_idx):
    ins = refs[:n_in]
    w_ref, x_ref, m_ref, o_ref = refs[n_in:]
    acc = None
    k0 = 0
    for a in ins:
        kk = a.shape[1]
        part = _dot(a[...], w_ref[k0:k0 + kk, :])
        acc = part if acc is None else acc + part
        k0 += kk
    o_ref[...] = x_ref[...] + m_ref[gate_idx:gate_idx + 1, :] * acc


def _outproj(ins, w, x, mod_l, row_fn, tm, tn, gate_idx):
    t, d = x.shape
    ktot = w.shape[0]
    in_specs = [pl.BlockSpec((tm, a.shape[1]), lambda i, j: (i, 0)) for a in ins]
    in_specs += [pl.BlockSpec((ktot, tn), lambda i, j: (0, j)),
                 pl.BlockSpec((tm, tn), lambda i, j: (i, j)),
                 pl.BlockSpec((None, 6, tn), lambda i, j: (row_fn(i), 0, j))]
    return pl.pallas_call(
        functools.partial(_outproj_body, n_in=len(ins), gate_idx=gate_idx),
        grid=(t // tm, d // tn), in_specs=in_specs,
        out_specs=pl.BlockSpec((tm, tn), lambda i, j: (i, j)),
        out_shape=jax.ShapeDtypeStruct((t, d), F32),
        compiler_params=_cparams(2), name="outproj_residual")(*ins, w, x, mod_l)


def _fill_padded(xpad_ref, x_ref):
    n = x_ref.shape[0]
    zeros = jnp.zeros((8, xpad_ref.shape[1]), F32)
    xpad_ref[0:8, :] = zeros
    xpad_ref[n + 8:n + 16, :] = zeros
    xpad_ref[8:n + 8, :] = x_ref[...].astype(F32)


def _conv_rows(xpad_ref, w, s, rows):
    acc = None
    for k in range(4):
        term = w[k:k + 1, :] * xpad_ref[s + 6 + k:s + 6 + k + rows, :]
        acc = term if acc is None else acc + term
    return acc


def _lru_body(gate_ref, x_ref, cw_ref, wa_ref, wi_ref, ba_ref, bi_ref, lam_ref, h0_ref,
              out_ref, st_ref, xpad, a_f, b_f, a_b, b_b, *, rows):
    n, cw = x_ref.shape
    nblk = cw // BLK_A
    _fill_padded(xpad, x_ref)
    w = cw_ref[...]
    scr = ((a_f, b_f), (a_b, b_b))
    for s in range(0, n, rows):
        xa = _conv_rows(xpad, w, s, rows)
        xab = xa.astype(BF16)
        for d in range(2):
            r_parts, i_parts = [], []
            for i in range(cw // BLK_A):
                xb = xab[:, i * BLK_A:(i + 1) * BLK_A]
                r_parts.append(_dot(xb, wa_ref[d, i].astype(BF16)))
                i_parts.append(_dot(xb, wi_ref[d, i].astype(BF16)))
            r = jax.nn.sigmoid(jnp.concatenate(r_parts, axis=1) + ba_ref[d:d + 1, :])
            ig = jax.nn.sigmoid(jnp.concatenate(i_parts, axis=1) + bi_ref[d:d + 1, :])
            log_a = (-LRU_C) * r * _softplus(-lam_ref[d:d + 1, :])
            a = jnp.exp(log_a)
            mult = jnp.sqrt(jnp.tanh(-log_a) * (a * a + 1.0))
            b = mult * (ig * xa)
            for i in range(nblk):
                scr[d][0][i, s:s + rows, :] = a[:, i * BLK_A:(i + 1) * BLK_A]
                scr[d][1][i, s:s + rows, :] = b[:, i * BLK_A:(i + 1) * BLK_A]

    ri = lax.broadcasted_iota(I32, (8, BLK_A), 0)
    nt = n // 8

    def tile_scan(a, b, carry, with_rows):
        for d in (1, 2, 4):
            keep = (ri >= d) if with_rows else (ri < 8 - d)
            shift = d if with_rows else 8 - d
            b = b + a * jnp.where(keep, pltpu.roll(b, shift, 0), 0.0)
            a = a * jnp.where(keep, pltpu.roll(a, shift, 0), 1.0)
        return b + a * carry

    def step(t, carry):
        tf = pl.ds(pl.multiple_of(t * 8, 8), 8)
        tb = pl.ds(pl.multiple_of((nt - 1 - t) * 8, 8), 8)
        out = []
        for j in range(nblk):
            hf = tile_scan(a_f[j, tf, :], b_f[j, tf, :], carry[2 * j], True)
            hb = tile_scan(a_b[j, tb, :], b_b[j, tb, :], carry[2 * j + 1], False)
            b_f[j, tf, :] = hf
            b_b[j, tb, :] = hb
            out += [hf[7:8, :], hb[0:1, :]]
        return tuple(out)

    init = []
    for j in range(nblk):
        init += [h0_ref[0, 0:1, j * BLK_A:(j + 1) * BLK_A], h0_ref[0, 1:2, j * BLK_A:(j + 1) * BLK_A]]
    last = lax.fori_loop(0, nt, step, tuple(init), unroll=2)
    st_ref[0, 0:1, :] = jnp.concatenate(last[0::2], axis=1)
    st_ref[0, 1:2, :] = jnp.concatenate(last[1::2], axis=1)
    for s in range(0, n, rows):
        hsum = jnp.concatenate([b_f[j, s:s + rows, :] + b_b[j, s:s + rows, :] for j in range(nblk)], axis=1)
        out_ref[s:s + rows, :] = (_gelu_tanh(gate_ref[s:s + rows, :]) * hsum).astype(BF16)


def _rglru(z_a, bsz, n, conv_w, wa, wi, ba, bi, lam, h0):
    d_a = conv_w.shape[1]
    cw = 256
    nb = d_a // cw
    hb = cw // BLK_A
    rows = min(n, 256)
    return pl.pallas_call(
        functools.partial(_lru_body, rows=rows), grid=(bsz, nb),
        in_specs=[pl.BlockSpec((n, cw), lambda b, j: (b, j)),
                  pl.BlockSpec((n, cw), lambda b, j: (b, nb + j)),
                  pl.BlockSpec((4, cw), lambda b, j: (0, j)),
                  pl.BlockSpec((2, hb, BLK_A, BLK_A), lambda b, j: (0, j, 0, 0)),
                  pl.BlockSpec((2, hb, BLK_A, BLK_A), lambda b, j: (0, j, 0, 0)),
                  pl.BlockSpec((2, cw), lambda b, j: (0, j)),
                  pl.BlockSpec((2, cw), lambda b, j: (0, j)),
                  pl.BlockSpec((2, cw), lambda b, j: (0, j)),
                  pl.BlockSpec((1, 2, cw), lambda b, j: (b, 0, j))],
        out_specs=[pl.BlockSpec((n, cw), lambda b, j: (b, j)),
                   pl.BlockSpec((1, 2, cw), lambda b, j: (b, 0, j))],
        out_shape=[jax.ShapeDtypeStruct((bsz * n, d_a), BF16),
                   jax.ShapeDtypeStruct((bsz, 2, d_a), F32)],
        scratch_shapes=[pltpu.VMEM((n + 16, cw), F32)] + [pltpu.VMEM((hb, n, BLK_A), F32)] * 4,
        compiler_params=_cparams(2), name="rglru")(z_a, z_a, conv_w, wa, wi, ba, bi, lam, h0)


def _ret_body(*refs, use_rope, has_state, chunk):
    q_ref, k_ref, v_ref, g_ref, dec_ref = refs[:5]
    pos = 5
    if use_rope:
        cos_ref, sin_ref = refs[pos:pos + 2]
        pos += 2
    if has_state:
        s0_ref = refs[pos]
        pos += 1
    o_ref, st_ref, rb_scr = refs[pos:pos + 3]
    n = q_ref.shape[0]
    c_len = chunk
    nc = n // c_len
    head = pl.program_id(1)

    lg = -_softplus(-dec_ref[...])
    lane = lax.broadcasted_iota(I32, lg.shape, 1)
    lgh = jnp.sum(jnp.where(lane == head, lg, 0.0), axis=1, keepdims=True)
    lgf, lgb = lgh[0:1, :], lgh[1:2, :]

    def prep(ref, c, sc):
        x = ref[c * c_len:(c + 1) * c_len, :].astype(F32) * sc
        if use_rope:
            x = (x * cos_ref[c * c_len:(c + 1) * c_len, :]
                 + pltpu.roll(x, DK_B // 2, 1) * sin_ref[c * c_len:(c + 1) * c_len, :])
        return x

    ii = lax.broadcasted_iota(I32, (c_len, 1), 0).astype(F32)
    jj = lax.broadcasted_iota(I32, (1, c_len), 1).astype(F32)
    rel = ii - jj
    dmask = jnp.where(rel > 0.0, jnp.exp(lgf * jnp.maximum(rel, 0.0)),
                      jnp.where(rel < 0.0, jnp.exp(lgb * jnp.maximum(-rel, 0.0)), 2.0))
    qf_dec = jnp.exp(lgf * (ii + 1.0))
    qb_dec = jnp.exp(lgb * (c_len - ii))
    kf_dec = jnp.exp(lgf * (c_len - 1.0 - ii))
    kb_dec = jnp.exp(lgb * ii)
    gf_c = jnp.exp(lgf * c_len)
    gb_c = jnp.exp(lgb * c_len)

    r_b = s0_ref[0, 1, 0] if has_state else jnp.zeros((DK_B, DV_B), F32)
    for c in range(nc - 1, -1, -1):
        rb_scr[c] = r_b
        k = prep(k_ref, c, 1.0)
        v = v_ref[c * c_len:(c + 1) * c_len, :].astype(BF16)
        r_b = r_b * gb_c + _dot_tn((k * kb_dec).astype(BF16), v)
    st_ref[0, 1, 0] = r_b

    r_f = s0_ref[0, 0, 0] if has_state else jnp.zeros((DK_B, DV_B), F32)
    for c in range(nc):
        q = prep(q_ref, c, DK_B ** -0.5)
        k = prep(k_ref, c, 1.0)
        v = v_ref[c * c_len:(c + 1) * c_len, :].astype(BF16)
        s = _dot_nt(q.astype(BF16), k.astype(BF16)) * dmask
        o = _dot(s.astype(BF16), v)
        qd = jnp.concatenate([q * qf_dec, q * qb_dec], axis=1).astype(BF16)
        rcat = jnp.concatenate([r_f, rb_scr[c]], axis=0).astype(BF16)
        o = o + _dot(qd, rcat)
        r_f = r_f * gf_c + _dot_tn((k * kf_dec).astype(BF16), v)
        on = o * lax.rsqrt(jnp.mean(o * o, axis=-1, keepdims=True) + EPS)
        g = g_ref[c * c_len:(c + 1) * c_len, :].astype(F32)
        o_ref[c * c_len:(c + 1) * c_len, :] = (_silu(g) * on).astype(BF16)
    st_ref[0, 0, 0] = r_f


def _retention(z_b, bsz, n, dec, rope, s0):
    nq = H_B
    chunk = min(n, RET_C)
    in_specs = [pl.BlockSpec((n, DK_B), lambda b, h: (b, h)),
                pl.BlockSpec((n, DK_B), lambda b, h: (b, nq + h)),
                pl.BlockSpec((n, DV_B), lambda b, h: (b, nq + h)),
                pl.BlockSpec((n, DV_B), lambda b, h: (b, 2 * nq + h)),
                pl.BlockSpec((2, H_B), lambda b, h: (0, 0))]
    args = [z_b, z_b, z_b, z_b, dec]
    if rope is not None:
        in_specs += [pl.BlockSpec((n, DK_B), lambda b, h: (0, 0))] * 2
        args += list(rope)
    if s0 is not None:
        in_specs.append(pl.BlockSpec((1, 2, 1, DK_B, DV_B), lambda b, h: (b, 0, h, 0, 0)))
        args.append(s0)
    return pl.pallas_call(
        functools.partial(_ret_body, use_rope=rope is not None, has_state=s0 is not None, chunk=chunk),
        grid=(bsz, H_B), in_specs=in_specs,
        out_specs=[pl.BlockSpec((n, DV_B), lambda b, h: (b, h)),
                   pl.BlockSpec((1, 2, 1, DK_B, DV_B), lambda b, h: (b, 0, h, 0, 0))],
        out_shape=[jax.ShapeDtypeStruct((bsz * n, H_B * DV_B), BF16),
                   jax.ShapeDtypeStruct((bsz, 2, H_B, DK_B, DV_B), F32)],
        scratch_shapes=[pltpu.VMEM((n // chunk, DK_B, DV_B), F32)],
        compiler_params=_cparams(2), name="retention")(*args)


def _rope_tables(n):
    rows = n // GRID_W
    n_freq = DK_B // 4
    row = jnp.repeat(jnp.arange(rows, dtype=F32), GRID_W)
    col = (jnp.arange(n) % GRID_W).astype(F32)
    inv = ROPE_BASE ** (-jnp.arange(n_freq, dtype=F32) / n_freq)
    ang = jnp.concatenate([row[:, None] * inv, col[:, None] * inv], axis=-1)
    cos, sin = jnp.cos(ang), jnp.sin(ang)
    return jnp.concatenate([cos, cos], axis=-1), jnp.concatenate([-sin, sin], axis=-1)


def _gdn_body(*refs, has_state, rows):
    (zq_ref, zk_ref, zv_ref, zg_ref, cq_ref, ck_ref, cv_ref, ab_ref, gr_ref,
     a8_ref, dt8_ref, ar_ref, dtr_ref, ng_ref) = refs[:14]
    pos = 14
    if has_state:
        s0_ref = refs[pos]
        pos += 1
    (o_ref, st_ref, xpad, xpad_v, q_scr, k_scr, v_scr, o_scr, bg_scr, grow_scr, s_scr,
     m_scr, blk_scr, u_scr, w_scr, at_scr, qd_scr, kd_scr, eg_scr) = refs[pos:]
    n = zq_ref.shape[0]
    c_len = GDN_C
    c4 = 4 * c_len
    nc = n // c_len
    lg = c_len.bit_length() - 1

    _fill_padded(xpad, zq_ref)
    for s in range(0, n, rows):
        q = _silu(_conv_rows(xpad, cq_ref[...], s, rows))
        q = q * lax.rsqrt(jnp.sum(q * q, axis=-1, keepdims=True) + EPS) * (DK_C ** -0.5)
        q_scr[s:s + rows, :] = q.astype(BF16)
    _fill_padded(xpad, zk_ref)
    for s in range(0, n, rows):
        k = _silu(_conv_rows(xpad, ck_ref[...], s, rows))
        k = k * lax.rsqrt(jnp.sum(k * k, axis=-1, keepdims=True) + EPS)
        k_scr[s:s + rows, :] = k.astype(BF16)
    _fill_padded(xpad_v, zv_ref)
    for s in range(0, n, rows):
        v_scr[s:s + rows, :] = _silu(_conv_rows(xpad_v, cv_ref[...], s, rows))

    ab = ab_ref[0, 0]
    gcols = -jnp.exp(a8_ref[0]) * _softplus(ab + dt8_ref[0])
    w_is_beta = ((lax.broadcasted_iota(I32, ab.shape, 1) >> 1) & 1) == 0
    bg_scr[...] = jnp.where(w_is_beta, jax.nn.sigmoid(ab), gcols)
    grow_scr[...] = -jnp.exp(ar_ref[0]) * _softplus(gr_ref[0, 0] + dtr_ref[0])
    o_scr[...] = jnp.zeros(o_scr.shape, F32)
    for d in range(2):
        if has_state:
            s_scr[d] = jnp.concatenate([s0_ref[0, d, 0], s0_ref[0, d, 1]], axis=0)
        else:
            s_scr[d] = jnp.zeros((2 * DK_C, DV_C), F32)

    ii = lax.broadcasted_iota(I32, (c_len, c4), 0)
    ll = lax.broadcasted_iota(I32, (c_len, c4), 1)
    jj = ll & (c_len - 1)
    lane_stream = ll >> lg
    fwd = lane_stream < 2
    one = lambda m: jnp.where(m, 1.0, 0.0)
    m_scr[0] = jnp.where(fwd, one(jj <= ii), one(jj >= ii))
    m_scr[1] = jnp.where(fwd, one(ii <= jj), one(ii >= jj))
    m_scr[2] = one((ii >> 3) == (jj >> 3))
    n_merge = lg - 3
    for t in range(n_merge):
        sh = 3 + t
        m_scr[3 + t] = jnp.where((ii >> (sh + 1)) == (jj >> (sh + 1)), one((ii >> sh) != (jj >> sh)), 0.0)
    eye = one(ii == jj)
    blk_scr[...] = one((lax.broadcasted_iota(I32, (c4, c4), 0) >> lg)
                       == (lax.broadcasted_iota(I32, (c4, c4), 1) >> lg)).astype(BF16)
    ci = lax.broadcasted_iota(I32, (c_len, c_len), 0)
    cj = lax.broadcasted_iota(I32, (c_len, c_len), 1)
    tri = (one(cj <= ci), one(cj >= ci))
    is_head0 = ((lax.broadcasted_iota(I32, (c4, DK_C), 0) >> lg) & 1) == 0

    def by_head(x):
        return jnp.concatenate([jnp.where(is_head0, x, 0.0), jnp.where(is_head0, 0.0, x)], axis=1).astype(BF16)

    def by_stream(cols):
        return jnp.where(fwd, jnp.where(lane_stream == 0, cols[0], cols[1]),
                         jnp.where(lane_stream == 2, cols[2], cols[3]))

    def block_diag(x):
        return jnp.concatenate([x.astype(BF16)] * 4, axis=0) * blk_scr[...]

    def tri_inverse(a_mats):
        a_d = [a * m_scr[2] for a in a_mats]
        a2 = [_dot(a.astype(BF16), block_diag(a)) for a in a_d]
        ps = [eye - a for a in a_d]
        both = [_dot(jnp.concatenate([p, x], axis=0).astype(BF16), block_diag(x)) for p, x in zip(ps, a2)]
        ps = [p + b[:c_len] for p, b in zip(ps, both)]
        ps = [p + _dot(p.astype(BF16), block_diag(b[c_len:])) for p, b in zip(ps, both)]
        for t in range(n_merge):
            zs = [_dot(p.astype(BF16), block_diag(a * m_scr[3 + t])) for p, a in zip(ps, a_mats)]
            ps = [p - _dot(z.astype(BF16), block_diag(p)) for p, z in zip(ps, zs)]
        return ps

    group = min(GDN_GROUP, nc)

    def prep_group(gi, carry):
        cs = [gi * group + t for t in range(group)]
        rows_c = [pl.ds(pl.multiple_of(c * c_len, c_len), c_len) for c in cs]
        out_rows = [pl.ds(pl.multiple_of(c * c4, c4), c4) for c in cs]
        kc = [k_scr[r, :] for r in rows_c]
        qc = [q_scr[r, :] for r in rows_c]
        kc4 = [jnp.concatenate([k] * 4, axis=0) for k in kc]
        qc4 = [jnp.concatenate([q] * 4, axis=0) for q in qc]
        gram = [_dot_nt(jnp.concatenate([k, q], axis=0), k4) for k, q, k4 in zip(kc, qc, kc4)]
        kk = [g[:c_len] for g in gram]
        qk = [g[c_len:] for g in gram]
        incl = m_scr[0]
        beta, gc_col, g_last, decay, a_mats = [], [], [], [], []
        for t in range(group):
            bg = bg_scr[rows_c[t], :]
            beta_s = [bg[:, 0:1], bg[:, 1:2], bg[:, 4:5], bg[:, 5:6]]
            gcol_s = [bg[:, 2:3], bg[:, 3:4], bg[:, 6:7], bg[:, 7:8]]
            grow = grow_scr[pl.ds(cs[t], 1), :]
            gc_s = [jnp.sum(tri[s // 2] * grow[:, s * c_len:(s + 1) * c_len], axis=1, keepdims=True)
                    for s in range(4)]
            gc_row = jnp.sum(m_scr[1] * by_stream(gcol_s), axis=0, keepdims=True)
            decay.append(jnp.exp((by_stream(gc_s) - gc_row) * incl) * incl)
            a_mats.append((by_stream(beta_s) * kk[t] * decay[t]) * (incl - eye))
            gl_s = [gc_s[s][c_len - 1:c_len, :] if s < 2 else gc_s[s][0:1, :] for s in range(4)]
            beta.append(jnp.concatenate(beta_s, axis=0))
            gc_col.append(jnp.concatenate(gc_s, axis=0))
            g_last.append(jnp.concatenate([jnp.broadcast_to(g, (c_len, 1)) for g in gl_s], axis=0))
        t_inv = tri_inverse(a_mats)
        blk2 = blk_scr[0:2 * c_len, 0:2 * c_len]
        for t in range(group):
            kf = kc4[t].astype(F32)
            qf = qc4[t].astype(F32)
            egc = jnp.exp(gc_col[t])
            vc = v_scr[rows_c[t], :]
            v4 = jnp.concatenate([vc[:, :DV_C], vc[:, DV_C:], vc[:, :DV_C], vc[:, DV_C:]], axis=0)
            kb = kf * (beta[t] * egc)
            rhs = jnp.concatenate([v4 * beta[t], kb], axis=1).astype(BF16)
            uw = _dot(block_diag(t_inv[t]), rhs)
            u_scr[out_rows[t], :] = uw[:, :DV_C]
            w = uw[:, DV_C:]
            w_scr[out_rows[t], :] = by_head(w)
            attn = (qk[t] * decay[t]).astype(BF16)
            at_scr[out_rows[t], :] = jnp.concatenate(
                [jnp.concatenate([attn[:, :2 * c_len]] * 2, axis=0) * blk2,
                 jnp.concatenate([attn[:, 2 * c_len:]] * 2, axis=0) * blk2], axis=0)
            qe = qf * egc
            qd_scr[out_rows[t], :] = by_head(qe)
            ke = kf * jnp.exp(g_last[t] - gc_col[t])
            kd_scr[out_rows[t], :] = by_head(ke)
            egl = jnp.exp(g_last[t])
            for s in range(4):
                eg_scr[pl.ds(cs[t] * 4 + s, 1), :] = jnp.broadcast_to(egl[s * c_len:s * c_len + 1, :], (1, DV_C))
        return carry

    lax.fori_loop(0, nc // group, prep_group, 0)

    def scan_chunk(c, carry):
        ccs = [c, nc - 1 - c]
        rows_d = [pl.ds(pl.multiple_of(ccs[d] * c4 + d * 2 * c_len, 2 * c_len), 2 * c_len) for d in range(2)]
        s_cat = [s_scr[d] for d in range(2)]
        sb = [s.astype(BF16) for s in s_cat]
        ws = [_dot(w_scr[rows_d[d], :], sb[d]) for d in range(2)]
        qs = [_dot(qd_scr[rows_d[d], :], sb[d]) for d in range(2)]
        vnb = [(u_scr[rows_d[d], :] - ws[d]).astype(BF16) for d in range(2)]
        kv = [_dot_tn(kd_scr[rows_d[d], :], vnb[d]) for d in range(2)]
        av = [_dot(at_scr[rows_d[d], :], vnb[d]) for d in range(2)]
        for d in range(2):
            e0 = jnp.broadcast_to(eg_scr[pl.ds(ccs[d] * 4 + 2 * d, 1), :], (DK_C, DV_C))
            e1 = jnp.broadcast_to(eg_scr[pl.ds(ccs[d] * 4 + 2 * d + 1, 1), :], (DK_C, DV_C))
            s_scr[d] = s_cat[d] * jnp.concatenate([e0, e1], axis=0) + kv[d]
            o = qs[d] + av[d]
            tok = pl.ds(pl.multiple_of(ccs[d] * c_len, c_len), c_len)
            o_scr[tok, 0:DV_C] += o[:c_len]
            o_scr[tok, DV_C:2 * DV_C] += o[c_len:]
        return carry

    lax.fori_loop(0, nc, scan_chunk, 0)

    for d in range(2):
        for r in range(2):
            st_ref[0, d, r] = s_scr[d, r * DK_C:(r + 1) * DK_C, :]
    ng = ng_ref[...]
    for s in range(0, n, rows):
        zg = zg_ref[s:s + rows, :].astype(F32)
        for r in range(2):
            o = o_scr[s:s + rows, r * DV_C:(r + 1) * DV_C]
            on = o * lax.rsqrt(jnp.mean(o * o, axis=-1, keepdims=True) + EPS) * ng
            o_ref[s:s + rows, r * DV_C:(r + 1) * DV_C] = (
                on * _silu(zg[:, r * DV_C:(r + 1) * DV_C])).astype(BF16)


def _gdn(z1, ab, bsz, n, conv_c, a_log, dt_bias, norm_g, s0):
    nc = n // GDN_C
    rows = min(n, 256)
    nqb = H_QK
    ab6 = ab.reshape(bsz, n, 2, 2, H_QK, 2)
    ab8 = ab6.transpose(0, 4, 1, 2, 3, 5).reshape(bsz, H_QK, n, 8)
    g_rows = ab6[:, :, :, 1].reshape(bsz, nc, GDN_C, 2, H_QK, 2).transpose(0, 4, 1, 3, 5, 2)
    g_rows = g_rows.reshape(bsz, H_QK, nc, 4 * GDN_C)

    def per_pair(p):
        p3 = p.astype(F32).reshape(2, H_QK, 2).transpose(1, 0, 2)
        return jnp.broadcast_to(p3[:, :, None, :], (H_QK, 2, 2, 2)).reshape(H_QK, 8)

    def per_stream(p):
        p3 = p.astype(F32).reshape(2, H_QK, 2).transpose(1, 0, 2).reshape(H_QK, 4)
        return jnp.repeat(p3, GDN_C, axis=1).reshape(H_QK, 1, 4 * GDN_C)

    a8, dt8 = per_pair(a_log), per_pair(dt_bias)
    in_specs = [pl.BlockSpec((n, DK_C), lambda b, j: (b, j)),
                pl.BlockSpec((n, DK_C), lambda b, j: (b, nqb + j)),
                pl.BlockSpec((n, 2 * DV_C), lambda b, j: (b, nqb + j)),
                pl.BlockSpec((n, 2 * DV_C), lambda b, j: (b, 2 * nqb + j)),
                pl.BlockSpec((4, DK_C), lambda b, j: (0, j)),
                pl.BlockSpec((4, DK_C), lambda b, j: (0, nqb + j)),
                pl.BlockSpec((4, 2 * DV_C), lambda b, j: (0, nqb + j)),
                pl.BlockSpec((1, 1, n, 8), lambda b, j: (b, j, 0, 0)),
                pl.BlockSpec((1, 1, nc, 4 * GDN_C), lambda b, j: (b, j, 0, 0)),
                pl.BlockSpec((1, 1, 8), lambda b, j: (j, 0, 0)),
                pl.BlockSpec((1, 1, 8), lambda b, j: (j, 0, 0)),
                pl.BlockSpec((1, 1, 4 * GDN_C), lambda b, j: (j, 0, 0)),
                pl.BlockSpec((1, 1, 4 * GDN_C), lambda b, j: (j, 0, 0)),
                pl.BlockSpec((1, DV_C), lambda b, j: (0, 0))]
    args = [z1, z1, z1, z1, conv_c, conv_c, conv_c, ab8, g_rows,
            a8.reshape(H_QK, 1, 8), dt8.reshape(H_QK, 1, 8),
            per_stream(a_log), per_stream(dt_bias), norm_g.reshape(1, DV_C)]
    if s0 is not None:
        in_specs.append(pl.BlockSpec((1, 2, 2, DK_C, DV_C), lambda b, j: (b, 0, j, 0, 0)))
        args.append(s0)
    return pl.pallas_call(
        functools.partial(_gdn_body, has_state=s0 is not None, rows=rows),
        grid=(bsz, H_QK), in_specs=in_specs,
        out_specs=[pl.BlockSpec((n, 2 * DV_C), lambda b, j: (b, j)),
                   pl.BlockSpec((1, 2, 2, DK_C, DV_C), lambda b, j: (b, 0, j, 0, 0))],
        out_shape=[jax.ShapeDtypeStruct((bsz * n, H_V * DV_C), BF16),
                   jax.ShapeDtypeStruct((bsz, 2, H_V, DK_C, DV_C), F32)],
        scratch_shapes=[pltpu.VMEM((n + 16, DK_C), F32), pltpu.VMEM((n + 16, 2 * DV_C), F32),
                        pltpu.VMEM((n, DK_C), BF16), pltpu.VMEM((n, DK_C), BF16),
                        pltpu.VMEM((n, 2 * DV_C), F32), pltpu.VMEM((n, 2 * DV_C), F32),
                        pltpu.VMEM((n, 8), F32), pltpu.VMEM((nc, 4 * GDN_C), F32),
                        pltpu.VMEM((2, 2 * DK_C, DV_C), F32),
                        pltpu.VMEM((GDN_C.bit_length() - 1, GDN_C, 4 * GDN_C), F32),
                        pltpu.VMEM((4 * GDN_C, 4 * GDN_C), BF16),
                        pltpu.VMEM((4 * n, DV_C), F32), pltpu.VMEM((4 * n, 2 * DK_C), BF16),
                        pltpu.VMEM((4 * n, 2 * GDN_C), BF16), pltpu.VMEM((4 * n, 2 * DK_C), BF16),
                        pltpu.VMEM((4 * n, 2 * DK_C), BF16), pltpu.VMEM((4 * nc, DV_C), F32)],
        compiler_params=_cparams(2), name="gated_deltanet")(*args)


def _router_body(x_ref, m_ref, g_ref, rwt_ref, h_ref, aff_ref):
    h = _adaln(x_ref[...], g_ref[...], m_ref[3:4, :], m_ref[4:5, :])
    h_ref[...] = h.astype(BF16)
    logits = _dot_nt(rwt_ref[...].astype(BF16), h.astype(BF16))
    ex = jnp.exp(logits - jnp.max(logits, axis=0, keepdims=True))
    aff_ref[0] = ex / jnp.sum(ex, axis=0, keepdims=True)


def _router(x, mod_l, row_fn, g, rw_t, bsz, n, tt):
    t, d = x.shape
    e = rw_t.shape[0]
    nt = n // tt
    return pl.pallas_call(
        _router_body, grid=(bsz, nt),
        in_specs=[pl.BlockSpec((tt, d), lambda b, i: (b * nt + i, 0)),
                  pl.BlockSpec((None, 6, d), lambda b, i: (row_fn(b), 0, 0)),
                  pl.BlockSpec((1, d), lambda b, i: (0, 0)),
                  pl.BlockSpec((e, d), lambda b, i: (0, 0))],
        out_specs=[pl.BlockSpec((tt, d), lambda b, i: (b * nt + i, 0)),
                   pl.BlockSpec((1, e, tt), lambda b, i: (b, 0, i))],
        out_shape=[jax.ShapeDtypeStruct((t, d), BF16), jax.ShapeDtypeStruct((bsz, e, n), F32)],
        compiler_params=_cparams(2), name="adaln_router")(x, mod_l, g.reshape(1, d), rw_t)


def _cumsum_lanes(m):
    r, n = m.shape
    blk = min(n, 256)
    tri = jnp.where(lax.broadcasted_iota(I32, (blk, blk), 0) <= lax.broadcasted_iota(I32, (blk, blk), 1),
                    1.0, 0.0).astype(BF16)
    outs = []
    run = jnp.zeros((r, 1), F32)
    for c in range(n // blk):
        loc = _dot(m[:, c * blk:(c + 1) * blk], tri) + run
        outs.append(loc)
        run = loc[:, blk - 1:blk]
    return outs[0] if len(outs) == 1 else jnp.concatenate(outs, axis=1)


def _topk_body(aff_ref, pos_ref, *, cap):
    bb, e, n = aff_ref.shape
    rows = bb * e
    bits = pltpu.bitcast(aff_ref[...].reshape(rows, n), I32)
    capf = float(cap)
    prefix = jnp.zeros((rows, 1), I32)
    for bit in range(30, -1, -1):
        cand = prefix | (1 << bit)
        cnt = jnp.sum(jnp.where(bits >= cand, 1.0, 0.0), axis=1, keepdims=True)
        prefix = jnp.where(cnt >= capf, cand, prefix)
    gt = bits > prefix
    eq = bits == prefix
    need = capf - jnp.sum(jnp.where(gt, 1.0, 0.0), axis=1, keepdims=True)
    eq_rank = _cumsum_lanes(jnp.where(eq, 1.0, 0.0).astype(BF16))
    sel = jnp.where(gt, 1.0, jnp.where(eq, jnp.where(eq_rank <= need, 1.0, 0.0), 0.0))
    slot = _cumsum_lanes(sel.astype(BF16)) - 1.0
    pos_ref[...] = jnp.where(sel > 0.5, slot, -1.0).astype(I32).reshape(bb, e, n)


def _topk(aff, cap):
    bsz, e, n = aff.shape
    return pl.pallas_call(
        functools.partial(_topk_body, cap=cap), grid=(1,),
        in_specs=[pl.BlockSpec((bsz, e, n), lambda i: (0, 0, 0))],
        out_specs=pl.BlockSpec((bsz, e, n), lambda i: (0, 0, 0)),
        out_shape=jax.ShapeDtypeStruct((bsz, e, n), I32),
        compiler_params=_cparams(1), name="expert_choice_topk")(aff)


def _gather_body(h_ref, pos_ref, aff_ref, xg_ref, gate_ref):
    eg, cap, _ = xg_ref.shape
    n = h_ref.shape[0]
    slot = lax.broadcasted_iota(I32, (cap, n), 0)
    for e in range(eg):
        hit = slot == pos_ref[0, e]
        onehot = jnp.where(hit, 1.0, 0.0).astype(BF16)
        xg_ref[e] = _dot(onehot, h_ref[...]).astype(BF16)
        gate_ref[e] = jnp.sum(jnp.where(hit, aff_ref[0, e], 0.0), axis=1, keepdims=True)


def _gather(h, pos4, aff4, bsz, n, cap, eg):
    d = h.shape[1]
    e = pos4.shape[1]
    return pl.pallas_call(
        _gather_body, grid=(bsz, e // eg),
        in_specs=[pl.BlockSpec((n, d), lambda b, g: (b, 0)),
                  pl.BlockSpec((1, eg, 1, n), lambda b, g: (b, g, 0, 0)),
                  pl.BlockSpec((1, eg, 1, n), lambda b, g: (b, g, 0, 0))],
        out_specs=[pl.BlockSpec((eg, cap, d), lambda b, g: (g, b, 0)),
                   pl.BlockSpec((eg, cap, 1), lambda b, g: (g, b, 0))],
        out_shape=[jax.ShapeDtypeStruct((e, bsz * cap, d), BF16),
                   jax.ShapeDtypeStruct((e, bsz * cap, 1), F32)],
        compiler_params=_cparams(2), name="moe_gather")(h, pos4, aff4)


def _ffn_body(x_ref, gate_ref, w1_ref, w3_ref, w2_ref, o_ref, hid, *, n_up):
    s = pl.program_id(1)
    i = pl.program_id(2)

    @pl.when(s < n_up)
    def _():
        x = x_ref[0]
        a = _dot(x, w1_ref[0].astype(BF16))
        b = _dot(x, w3_ref[0].astype(BF16))
        hid[i, s] = (_silu(a) * b).astype(BF16)

    @pl.when(s >= n_up)
    def _():
        h = jnp.concatenate([hid[i, k] for k in range(n_up)], axis=1)
        o_ref[0] = (_dot(h, w2_ref[0].astype(BF16)) * gate_ref[0]).astype(BF16)


def _ffn(xg, gate, w1, w3, w2, layer, tm, tf):
    e, m, d = xg.shape
    ff = w1.shape[3]
    n_up = ff // tf
    nm = m // tm
    up = lambda s: jnp.minimum(s, n_up - 1)
    down = lambda s: jnp.maximum(s - n_up, 0)
    row_up = lambda s, i: jnp.where(s < n_up, i, nm - 1)
    row_dn = lambda s, i: jnp.where(s < n_up, 0, i)
    return pl.pallas_call(
        functools.partial(_ffn_body, n_up=n_up), grid=(e, n_up + d // tf, nm),
        in_specs=[pl.BlockSpec((1, tm, d), lambda x, s, i: (x, row_up(s, i), 0)),
                  pl.BlockSpec((1, tm, 1), lambda x, s, i: (x, row_dn(s, i), 0)),
                  pl.BlockSpec((None, 1, d, tf), lambda x, s, i: (layer, x, 0, up(s))),
                  pl.BlockSpec((None, 1, d, tf), lambda x, s, i: (layer, x, 0, up(s))),
                  pl.BlockSpec((None, 1, ff, tf), lambda x, s, i: (layer, x, 0, down(s)))],
        out_specs=pl.BlockSpec((1, tm, tf), lambda x, s, i: (x, row_dn(s, i), down(s))),
        out_shape=jax.ShapeDtypeStruct((e, m, d), BF16),
        scratch_shapes=[pltpu.VMEM((nm, n_up, tm, tf), BF16)],
        compiler_params=_cparams(3), name="expert_ffn")(xg, gate, w1, w3, w2)


def _combine_body(pos_ref, y_ref, x_ref, m_ref, fg_ref, o_ref, acc, *, final):
    e = pl.program_id(2)
    eg, cap, d = y_ref.shape
    tt = x_ref.shape[0]
    rows8 = [pos_ref[0, g] for g in range(eg)] + [jnp.zeros((1, tt), I32)] * ((-eg) % 8)
    pos_rows = jnp.concatenate(rows8, axis=0).astype(F32).astype(BF16)
    eye = jnp.where(lax.broadcasted_iota(I32, (tt, tt), 0) == lax.broadcasted_iota(I32, (tt, tt), 1),
                    1.0, 0.0).astype(BF16)
    pos_cols = _dot_nt(eye, pos_rows)
    slot = lax.broadcasted_iota(I32, (tt, cap), 1).astype(F32)
    onehot = jnp.concatenate([jnp.where(slot == pos_cols[:, g:g + 1], 1.0, 0.0).astype(BF16)
                              for g in range(eg)], axis=1)
    part = _dot(onehot, y_ref[...].reshape(eg * cap, d))

    @pl.when(e == 0)
    def _():
        acc[...] = part

    @pl.when(e > 0)
    def _():
        acc[...] += part

    @pl.when(e == pl.num_programs(2) - 1)
    def _():
        xn = x_ref[...] + m_ref[5:6, :] * acc[...]
        if final:
            xn = xn * lax.rsqrt(jnp.mean(xn * xn, axis=-1, keepdims=True) + EPS) * fg_ref[...]
        o_ref[...] = xn


def _combine(pos4, y, y_off, x, mod_l, row_fn, final_g, bsz, n, cap, tt, eg, final):
    t, d = x.shape
    e = y.shape[0]
    nt = n // tt
    return pl.pallas_call(
        functools.partial(_combine_body, final=final), grid=(bsz, nt, e // eg),
        in_specs=[pl.BlockSpec((1, eg, 1, tt), lambda b, i, x_: (b, x_, 0, i)),
                  pl.BlockSpec((eg, cap, d), lambda b, i, x_: (x_, y_off + b, 0)),
                  pl.BlockSpec((tt, d), lambda b, i, x_: (b * nt + i, 0)),
                  pl.BlockSpec((None, 6, d), lambda b, i, x_: (row_fn(b), 0, 0)),
                  pl.BlockSpec((1, d), lambda b, i, x_: (0, 0))],
        out_specs=pl.BlockSpec((tt, d), lambda b, i, x_: (b * nt + i, 0)),
        out_shape=jax.ShapeDtypeStruct((t, d), F32),
        scratch_shapes=[pltpu.VMEM((tt, d), F32)],
        compiler_params=_cparams(3), name="moe_combine")(pos4, y, x, mod_l, final_g.reshape(1, d))


def _ec_moe_layer(groups, mod_l, l, p, final):
    e = p["router_t"].shape[1]
    routed = []
    for g in groups:
        bsz, n = g["bsz"], g["n"]
        cap = EC_FACTOR * n // e
        tt = min(n, 512)
        h, aff = _router(g["x"], mod_l, g["req_row"], p["norm2_g"][l], p["router_t"][l], bsz, n, tt)
        pos4 = _topk(aff, cap).reshape(bsz, e, 1, n)
        eg = e if cap * e <= 1024 else 1
        xg, gate = _gather(h, pos4, aff.reshape(bsz, e, 1, n), bsz, n, cap, eg)
        routed.append((pos4, xg, gate, cap, tt))
    xg_all = jnp.concatenate([r[1] for r in routed], axis=1)
    gate_all = jnp.concatenate([r[2] for r in routed], axis=1)
    m = xg_all.shape[1]
    tm = m if m <= 1024 else 1024
    y = _ffn(xg_all, gate_all, p["exp_w1"], p["exp_w3"], p["exp_w2"], l, tm, 512)
    row0 = 0
    for g, (pos4, xg, _, cap, tt) in zip(groups, routed):
        assert row0 % cap == 0, "a group's rows must start on a multiple of its expert capacity"
        eg_c = max(1, min(e, 1024 // cap))
        g["x"] = _combine(pos4, y, row0 // cap, g["x"], mod_l, g["req_row"], p["final_g"], g["bsz"], g["n"],
                          cap, tt, eg_c, final)
        row0 += xg.shape[1]


def _mixer_layer(g, mod_l, l, p):
    x, bsz, n, tm, tile_row = g["x"], g["bsz"], g["n"], g["tm"], g["tile_row"]
    s_lru, s_ret, s_gdn = g["states"]
    with_rope = g["rope"]
    outs = g["outs"]
    if l % 2 == 0:
        e = l // 2
        w_in = p["w_in0"][e]
        d_a = p["conv_a"].shape[-1]
        z_a = _inproj(x, mod_l, tile_row, p["norm1_g"][l], w_in, 0, 2 * d_a, F32, tm, 512, 0, 1)
        z_b = _inproj(x, mod_l, tile_row, p["norm1_g"][l], w_in, 2 * d_a, w_in.shape[1] - 2 * d_a,
                      ACT_DTYPE, tm, 512, 0, 1)
        h0 = jnp.zeros((bsz, 2, d_a), F32) if s_lru is None else s_lru[:, e]
        out_a, st_a = _rglru(z_a, bsz, n, p["conv_a"][e], p["lru_wa"][e], p["lru_wi"][e],
                             p["lru_ba"][e], p["lru_bi"][e], p["lru_lam"][e], h0)
        out_b, st_b = _retention(z_b, bsz, n, p["ret_decay"][e], _rope_tables(n) if with_rope else None,
                                 None if s_ret is None else s_ret[:, e])
        outs.setdefault("lru", []).append(st_a)
        outs.setdefault("ret", []).append(st_b)
        x = _outproj([out_a, out_b], p["w_out0"][e], x, mod_l, tile_row, tm, 512, 2)
    else:
        o = l // 2
        w_in = p["w_in1"][o]
        nz = H_QK * DK_C * 2 + 2 * H_V * DV_C
        z1 = _inproj(x, mod_l, tile_row, p["norm1_g"][l], w_in, 0, nz, ACT_DTYPE, tm, 512, 0, 1)
        ab = _inproj(x, mod_l, tile_row, p["norm1_g"][l], w_in, nz, 4 * H_V, F32, tm, 4 * H_V, 0, 1)
        out_c, st_c = _gdn(z1, ab, bsz, n, p["conv_c"][o], p["gdn_a_log"][o], p["gdn_dt_bias"][o],
                           p["gdn_norm_g"][o], None if s_gdn is None else s_gdn[:, o])
        outs.setdefault("gdn", []).append(st_c)
        x = _outproj([out_c], p["w_out1"][o], x, mod_l, tile_row, tm, 512, 2)
    g["x"] = x


def _group(x3, row_of_request, states, rope):
    bsz, n, d = x3.shape
    if row_of_request is None:
        tm = min(bsz * n, 1024)
        tile_row = lambda i: 0
        req_row = lambda b: 0
    else:
        tm = min(n, 1024)
        tiles_per_req = n // tm
        tile_row = lambda i: row_of_request + i // tiles_per_req
        req_row = lambda b: row_of_request + b
    return dict(x=x3.reshape(bsz * n, d), bsz=bsz, n=n, tm=tm, tile_row=tile_row, req_row=req_row,
                states=states, rope=rope, outs={})


def kernel(x_prompt, x_sample, state_rglru, state_ret, state_gdn, c, c_ctx, mod_w, mod_b, norm1_g, norm2_g, w_in0, conv_a, lru_wa, lru_ba, lru_wi, lru_bi, lru_lam, ret_decay, w_out0, w_in1, conv_c, gdn_a_log, gdn_dt_bias, gdn_norm_g, w_out1, router_w, exp_w1, exp_w3, exp_w2, final_g):
    pad = (-(c.shape[0] + 1)) % 8
    cond = jnp.concatenate([c_ctx[None, :], c, jnp.zeros((pad, c.shape[1]), F32)], axis=0)
    mod = _modulation(cond, mod_w, mod_b)
    p = dict(mod=mod, norm1_g=norm1_g, norm2_g=norm2_g,
             w_in0=w_in0.astype(BF16), conv_a=conv_a, lru_wa=lru_wa, lru_ba=lru_ba, lru_wi=lru_wi,
             lru_bi=lru_bi, lru_lam=lru_lam, ret_decay=ret_decay, w_out0=w_out0.astype(BF16),
             w_in1=w_in1.astype(BF16), conv_c=conv_c, gdn_a_log=gdn_a_log, gdn_dt_bias=gdn_dt_bias,
             gdn_norm_g=gdn_norm_g, w_out1=w_out1.astype(BF16), router_t=jnp.swapaxes(router_w, 1, 2),
             exp_w1=exp_w1, exp_w3=exp_w3, exp_w2=exp_w2, final_g=final_g)
    groups = [_group(x_prompt, None, (None, None, None), False),
              _group(x_sample, 1, (state_rglru, state_ret, state_gdn), True)]
    depth = mod.shape[0]
    for l in range(depth):
        for g in groups:
            _mixer_layer(g, mod[l], l, p)
        _ec_moe_layer(groups, mod[l], l, p, l == depth - 1)
    st = groups[0]["outs"]
    return (groups[0]["x"].reshape(x_prompt.shape), groups[1]["x"].reshape(x_sample.shape),
            jnp.stack(st["lru"], axis=1), jnp.stack(st["ret"], axis=1), jnp.stack(st["gdn"], axis=1))
```

```python
import functools

import jax
import jax.numpy as jnp
from jax import lax
from jax.experimental import pallas as pl
from jax.experimental.pallas import tpu as pltpu

F32 = jnp.float32
BF16 = jnp.bfloat16
I32 = jnp.int32
ACT_DTYPE = BF16
EPS = 1e-6
GRID_W = 64
LRU_C = 8.0
ROPE_BASE = 10000.0
H_A, BLK_A = 16, 128
H_B, DK_B, DV_B = 8, 128, 256
H_QK, H_V, DK_C, DV_C = 16, 32, 128, 128
GDN_C = 64
GDN_GROUP = 4
RET_C = 256
N_EXPERTS = 16
EC_FACTOR = 2
V7X_VMEM_LIMIT = 56 * 1024 * 1024


def _cparams(n_axes):
    return pltpu.CompilerParams(dimension_semantics=("arbitrary",) * n_axes,
                                vmem_limit_bytes=V7X_VMEM_LIMIT)


def _silu(x):
    return x * jax.nn.sigmoid(x)


def _softplus(x):
    return jnp.maximum(x, 0.0) + jnp.log1p(jnp.exp(-jnp.abs(x)))


def _gelu_tanh(x):
    return 0.5 * x * (1.0 + jnp.tanh(0.7978845608028654 * (x + 0.044715 * (x * x * x))))


def _adaln(x, g, shift, scale):
    r = lax.rsqrt(jnp.mean(x * x, axis=-1, keepdims=True) + EPS)
    return x * r * g * (1.0 + scale) + shift


def _dot(a, b):
    return jnp.dot(a, b, preferred_element_type=F32)


def _dot_nt(a, b):
    return lax.dot_general(a, b, (((1,), (1,)), ((), ())), preferred_element_type=F32)


def _dot_tn(a, b):
    return lax.dot_general(a, b, (((0,), (0,)), ((), ())), preferred_element_type=F32)


def _mod_body(c_ref, w_ref, b_ref, o_ref):
    s = _silu(c_ref[...]).astype(BF16)
    o_ref[0] = _dot(s, w_ref[0].astype(BF16)) + b_ref[0]


def _modulation(cond, mod_w, mod_b):
    n_l, d, d6 = mod_w.shape
    r = cond.shape[0]
    tn = 1024
    out = pl.pallas_call(
        _mod_body, grid=(n_l, d6 // tn),
        in_specs=[pl.BlockSpec((r, d), lambda l, j: (0, 0)),
                  pl.BlockSpec((1, d, tn), lambda l, j: (l, 0, j)),
                  pl.BlockSpec((1, 1, tn), lambda l, j: (l, 0, j))],
        out_specs=pl.BlockSpec((1, r, tn), lambda l, j: (l, 0, j)),
        out_shape=jax.ShapeDtypeStruct((n_l, r, d6), F32),
        compiler_params=_cparams(2), name="modulation")(cond, mod_w, mod_b.reshape(n_l, 1, d6))
    return out.reshape(n_l, r, 6, d)


def _inproj_body(x_ref, m_ref, g_ref, w_ref, o_ref, h_scr, *, shift_idx, scale_idx):
    @pl.when(pl.program_id(1) == 0)
    def _():
        h = _adaln(x_ref[...], g_ref[...], m_ref[shift_idx:shift_idx + 1, :], m_ref[scale_idx:scale_idx + 1, :])
        h_scr[...] = h.astype(BF16)
    o_ref[...] = _dot(h_scr[...], w_ref[...]).astype(o_ref.dtype)


def _inproj(x, mod_l, row_fn, g, w, col0, ncols, out_dtype, tm, tn, shift_idx, scale_idx):
    t, d = x.shape
    c0 = col0 // tn
    return pl.pallas_call(
        functools.partial(_inproj_body, shift_idx=shift_idx, scale_idx=scale_idx),
        grid=(t // tm, ncols // tn),
        in_specs=[pl.BlockSpec((tm, d), lambda i, j: (i, 0)),
                  pl.BlockSpec((None, 6, d), lambda i, j: (row_fn(i), 0, 0)),
                  pl.BlockSpec((1, d), lambda i, j: (0, 0)),
                  pl.BlockSpec((d, tn), lambda i, j: (0, c0 + j))],
        out_specs=pl.BlockSpec((tm, tn), lambda i, j: (i, j)),
        out_shape=jax.ShapeDtypeStruct((t, ncols), out_dtype),
        scratch_shapes=[pltpu.VMEM((tm, d), BF16)],
        compiler_params=_cparams(2), name="adaln_inproj")(x, mod_l, g.reshape(1, d), w)


def _outproj_body(*refs, n_in, gate_idx):
    ins = refs[:n_in]
    w_ref, x_ref, m_ref, o_ref = refs[n_in:]
    acc = None
    k0 = 0
    for a in ins:
        kk = a.shape[1]
        part = _dot(a[...], w_ref[k0:k0 + kk, :])
        acc = part if acc is None else acc + part
        k0 += kk
    o_ref[...] = x_ref[...] + m_ref[gate_idx:gate_idx + 1, :] * acc


def _outproj(ins, w, x, mod_l, row_fn, tm, tn, gate_idx):
    t, d = x.shape
    ktot = w.shape[0]
    in_specs = [pl.BlockSpec((tm, a.shape[1]), lambda i, j: (i, 0)) for a in ins]
    in_specs += [pl.BlockSpec((ktot, tn), lambda i, j: (0, j)),
                 pl.BlockSpec((tm, tn), lambda i, j: (i, j)),
                 pl.BlockSpec((None, 6, tn), lambda i, j: (row_fn(i), 0, j))]
    return pl.pallas_call(
        functools.partial(_outproj_body, n_in=len(ins), gate_idx=gate_idx),
        grid=(t // tm, d // tn), in_specs=in_specs,
        out_specs=pl.BlockSpec((tm, tn), lambda i, j: (i, j)),
        out_shape=jax.ShapeDtypeStruct((t, d), F32),
        compiler_params=_cparams(2), name="outproj_residual")(*ins, w, x, mod_l)


def _fill_padded(xpad_ref, x_ref):
    n = x_ref.shape[0]
    zeros = jnp.zeros((8, xpad_ref.shape[1]), F32)
    xpad_ref[0:8, :] = zeros
    xpad_ref[n + 8:n + 16, :] = zeros
    xpad_ref[8:n + 8, :] = x_ref[...].astype(F32)


def _conv_rows(xpad_ref, w, s, rows):
    acc = None
    for k in range(4):
        term = w[k:k + 1, :] * xpad_ref[s + 6 + k:s + 6 + k + rows, :]
        acc = term if acc is None else acc + term
    return acc


def _lru_body(gate_ref, x_ref, cw_ref, wa_ref, wi_ref, ba_ref, bi_ref, lam_ref, h0_ref,
              out_ref, st_ref, xpad, a_f, b_f, a_b, b_b, *, rows):
    n, cw = x_ref.shape
    nblk = cw // BLK_A
    _fill_padded(xpad, x_ref)
    w = cw_ref[...]
    scr = ((a_f, b_f), (a_b, b_b))
    for s in range(0, n, rows):
        xa = _conv_rows(xpad, w, s, rows)
        xab = xa.astype(BF16)
        for d in range(2):
            r_parts, i_parts = [], []
            for i in range(cw // BLK_A):
                xb = xab[:, i * BLK_A:(i + 1) * BLK_A]
                r_parts.append(_dot(xb, wa_ref[d, i].astype(BF16)))
                i_parts.append(_dot(xb, wi_ref[d, i].astype(BF16)))
            r = jax.nn.sigmoid(jnp.concatenate(r_parts, axis=1) + ba_ref[d:d + 1, :])
            ig = jax.nn.sigmoid(jnp.concatenate(i_parts, axis=1) + bi_ref[d:d + 1, :])
            log_a = (-LRU_C) * r * _softplus(-lam_ref[d:d + 1, :])
            a = jnp.exp(log_a)
            mult = jnp.sqrt(jnp.tanh(-log_a) * (a * a + 1.0))
            b = mult * (ig * xa)
            for i in range(nblk):
                scr[d][0][i, s:s + rows, :] = a[:, i * BLK_A:(i + 1) * BLK_A]
                scr[d][1][i, s:s + rows, :] = b[:, i * BLK_A:(i + 1) * BLK_A]

    ri = lax.broadcasted_iota(I32, (8, BLK_A), 0)
    nt = n // 8

    def tile_scan(a, b, carry, with_rows):
        for d in (1, 2, 4):
            keep = (ri >= d) if with_rows else (ri < 8 - d)
            shift = d if with_rows else 8 - d
            b = b + a * jnp.where(keep, pltpu.roll(b, shift, 0), 0.0)
            a = a * jnp.where(keep, pltpu.roll(a, shift, 0), 1.0)
        return b + a * carry

    def step(t, carry):
        tf = pl.ds(pl.multiple_of(t * 8, 8), 8)
        tb = pl.ds(pl.multiple_of((nt - 1 - t) * 8, 8), 8)
        out = []
        for j in range(nblk):
            hf = tile_scan(a_f[j, tf, :], b_f[j, tf, :], carry[2 * j], True)
            hb = tile_scan(a_b[j, tb, :], b_b[j, tb, :], carry[2 * j + 1], False)
            b_f[j, tf, :] = hf
            b_b[j, tb, :] = hb
            out += [hf[7:8, :], hb[0:1, :]]
        return tuple(out)

    init = []
    for j in range(nblk):
        init += [h0_ref[0, 0:1, j * BLK_A:(j + 1) * BLK_A], h0_ref[0, 1:2, j * BLK_A:(j + 1) * BLK_A]]
    last = lax.fori_loop(0, nt, step, tuple(init), unroll=2)
    st_ref[0, 0:1, :] = jnp.concatenate(last[0::2], axis=1)
    st_ref[0, 1:2, :] = jnp.concatenate(last[1::2], axis=1)
    for s in range(0, n, rows):
        hsum = jnp.concatenate([b_f[j, s:s + rows, :] + b_b[j, s:s + rows, :] for j in range(nblk)], axis=1)
        out_ref[s:s + rows, :] = (_gelu_tanh(gate_ref[s:s + rows, :]) * hsum).astype(BF16)


def _rglru(z_a, bsz, n, conv_w, wa, wi, ba, bi, lam, h0):
    d_a = conv_w.shape[1]
    cw = 256
    nb = d_a // cw
    hb = cw // BLK_A
    rows = min(n, 256)
    return pl.pallas_call(
        functools.partial(_lru_body, rows=rows), grid=(bsz, nb),
        in_specs=[pl.BlockSpec((n, cw), lambda b, j: (b, j)),
                  pl.BlockSpec((n, cw), lambda b, j: (b, nb + j)),
                  pl.BlockSpec((4, cw), lambda b, j: (0, j)),
                  pl.BlockSpec((2, hb, BLK_A, BLK_A), lambda b, j: (0, j, 0, 0)),
                  pl.BlockSpec((2, hb, BLK_A, BLK_A), lambda b, j: (0, j, 0, 0)),
                  pl.BlockSpec((2, cw), lambda b, j: (0, j)),
                  pl.BlockSpec((2, cw), lambda b, j: (0, j)),
                  pl.BlockSpec((2, cw), lambda b, j: (0, j)),
                  pl.BlockSpec((1, 2, cw), lambda b, j: (b, 0, j))],
        out_specs=[pl.BlockSpec((n, cw), lambda b, j: (b, j)),
                   pl.BlockSpec((1, 2, cw), lambda b, j: (b, 0, j))],
        out_shape=[jax.ShapeDtypeStruct((bsz * n, d_a), BF16),
                   jax.ShapeDtypeStruct((bsz, 2, d_a), F32)],
        scratch_shapes=[pltpu.VMEM((n + 16, cw), F32)] + [pltpu.VMEM((hb, n, BLK_A), F32)] * 4,
        compiler_params=_cparams(2), name="rglru")(z_a, z_a, conv_w, wa, wi, ba, bi, lam, h0)


def _ret_body(*refs, use_rope, has_state, chunk):
    q_ref, k_ref, v_ref, g_ref, dec_ref = refs[:5]
    pos = 5
    if use_rope:
        cos_ref, sin_ref = refs[pos:pos + 2]
        pos += 2
    if has_state:
        s0_ref = refs[pos]
        pos += 1
    o_ref, st_ref, rb_scr = refs[pos:pos + 3]
    n = q_ref.shape[0]
    c_len = chunk
    nc = n // c_len
    head = pl.program_id(1)

    lg = -_softplus(-dec_ref[...])
    lane = lax.broadcasted_iota(I32, lg.shape, 1)
    lgh = jnp.sum(jnp.where(lane == head, lg, 0.0), axis=1, keepdims=True)
    lgf, lgb = lgh[0:1, :], lgh[1:2, :]

    def prep(ref, c, sc):
        x = ref[c * c_len:(c + 1) * c_len, :].astype(F32) * sc
        if use_rope:
            x = (x * cos_ref[c * c_len:(c + 1) * c_len, :]
                 + pltpu.roll(x, DK_B // 2, 1) * sin_ref[c * c_len:(c + 1) * c_len, :])
        return x

    ii = lax.broadcasted_iota(I32, (c_len, 1), 0).astype(F32)
    jj = lax.broadcasted_iota(I32, (1, c_len), 1).astype(F32)
    rel = ii - jj
    dmask = jnp.where(rel > 0.0, jnp.exp(lgf * jnp.maximum(rel, 0.0)),
                      jnp.where(rel < 0.0, jnp.exp(lgb * jnp.maximum(-rel, 0.0)), 2.0))
    qf_dec = jnp.exp(lgf * (ii + 1.0))
    qb_dec = jnp.exp(lgb * (c_len - ii))
    kf_dec = jnp.exp(lgf * (c_len - 1.0 - ii))
    kb_dec = jnp.exp(lgb * ii)
    gf_c = jnp.exp(lgf * c_len)
    gb_c = jnp.exp(lgb * c_len)

    r_b = s0_ref[0, 1, 0] if has_state else jnp.zeros((DK_B, DV_B), F32)
    for c in range(nc - 1, -1, -1):
        rb_scr[c] = r_b
        k = prep(k_ref, c, 1.0)
        v = v_ref[c * c_len:(c + 1) * c_len, :].astype(BF16)
        r_b = r_b * gb_c + _dot_tn((k * kb_dec).astype(BF16), v)
    st_ref[0, 1, 0] = r_b

    r_f = s0_ref[0, 0, 0] if has_state else jnp.zeros((DK_B, DV_B), F32)
    for c in range(nc):
        q = prep(q_ref, c, DK_B ** -0.5)
        k = prep(k_ref, c, 1.0)
        v = v_ref[c * c_len:(c + 1) * c_len, :].astype(BF16)
        s = _dot_nt(q.astype(BF16), k.astype(BF16)) * dmask
        o = _dot(s.astype(BF16), v)
        qd = jnp.concatenate([q * qf_dec, q * qb_dec], axis=1).astype(BF16)
        rcat = jnp.concatenate([r_f, rb_scr[c]], axis=0).astype(BF16)
        o = o + _dot(qd, rcat)
        r_f = r_f * gf_c + _dot_tn((k * kf_dec).astype(BF16), v)
        on = o * lax.rsqrt(jnp.mean(o * o, axis=-1, keepdims=True) + EPS)
        g = g_ref[c * c_len:(c + 1) * c_len, :].astype(F32)
        o_ref[c * c_len:(c + 1) * c_len, :] = (_silu(g) * on).astype(BF16)
    st_ref[0, 0, 0] = r_f


def _retention(z_b, bsz, n, dec, rope, s0):
    nq = H_B
    chunk = min(n, RET_C)
    in_specs = [pl.BlockSpec((n, DK_B), lambda b, h: (b, h)),
                pl.BlockSpec((n, DK_B), lambda b, h: (b, nq + h)),
                pl.BlockSpec((n, DV_B), lambda b, h: (b, nq + h)),
                pl.BlockSpec((n, DV_B), lambda b, h: (b, 2 * nq + h)),
                pl.BlockSpec((2, H_B), lambda b, h: (0, 0))]
    args = [z_b, z_b, z_b, z_b, dec]
    if rope is not None:
        in_specs += [pl.BlockSpec((n, DK_B), lambda b, h: (0, 0))] * 2
        args += list(rope)
    if s0 is not None:
        in_specs.append(pl.BlockSpec((1, 2, 1, DK_B, DV_B), lambda b, h: (b, 0, h, 0, 0)))
        args.append(s0)
    return pl.pallas_call(
        functools.partial(_ret_body, use_rope=rope is not None, has_state=s0 is not None, chunk=chunk),
        grid=(bsz, H_B), in_specs=in_specs,
        out_specs=[pl.BlockSpec((n, DV_B), lambda b, h: (b, h)),
                   pl.BlockSpec((1, 2, 1, DK_B, DV_B), lambda b, h: (b, 0, h, 0, 0))],
        out_shape=[jax.ShapeDtypeStruct((bsz * n, H_B * DV_B), BF16),
                   jax.ShapeDtypeStruct((bsz, 2, H_B, DK_B, DV_B), F32)],
        scratch_shapes=[pltpu.VMEM((n // chunk, DK_B, DV_B), F32)],
        compiler_params=_cparams(2), name="retention")(*args)


def _rope_tables(n):
    rows = n // GRID_W
    n_freq = DK_B // 4
    row = jnp.repeat(jnp.arange(rows, dtype=F32), GRID_W)
    col = (jnp.arange(n) % GRID_W).astype(F32)
    inv = ROPE_BASE ** (-jnp.arange(n_freq, dtype=F32) / n_freq)
    ang = jnp.concatenate([row[:, None] * inv, col[:, None] * inv], axis=-1)
    cos, sin = jnp.cos(ang), jnp.sin(ang)
    return jnp.concatenate([cos, cos], axis=-1), jnp.concatenate([-sin, sin], axis=-1)


def _gdn_body(*refs, has_state, rows):
    (zq_ref, zk_ref, zv_ref, zg_ref, cq_ref, ck_ref, cv_ref, ab_ref, gr_ref,
     a8_ref, dt8_ref, ar_ref, dtr_ref, ng_ref) = refs[:14]
    pos = 14
    if has_state:
        s0_ref = refs[pos]
        pos += 1
    (o_ref, st_ref, xpad, xpad_v, q_scr, k_scr, v_scr, o_scr, bg_scr, grow_scr, s_scr,
     m_scr, blk_scr, u_scr, w_scr, at_scr, qd_scr, kd_scr, eg_scr) = refs[pos:]
    n = zq_ref.shape[0]
    c_len = GDN_C
    c4 = 4 * c_len
    nc = n // c_len
    lg = c_len.bit_length() - 1

    _fill_padded(xpad, zq_ref)
    for s in range(0, n, rows):
        q = _silu(_conv_rows(xpad, cq_ref[...], s, rows))
        q = q * lax.rsqrt(jnp.sum(q * q, axis=-1, keepdims=True) + EPS) * (DK_C ** -0.5)
        q_scr[s:s + rows, :] = q.astype(BF16)
    _fill_padded(xpad, zk_ref)
    for s in range(0, n, rows):
        k = _silu(_conv_rows(xpad, ck_ref[...], s, rows))
        k = k * lax.rsqrt(jnp.sum(k * k, axis=-1, keepdims=True) + EPS)
        k_scr[s:s + rows, :] = k.astype(BF16)
    _fill_padded(xpad_v, zv_ref)
    for s in range(0, n, rows):
        v_scr[s:s + rows, :] = _silu(_conv_rows(xpad_v, cv_ref[...], s, rows))

    ab = ab_ref[0, 0]
    gcols = -jnp.exp(a8_ref[0]) * _softplus(ab + dt8_ref[0])
    w_is_beta = ((lax.broadcasted_iota(I32, ab.shape, 1) >> 1) & 1) == 0
    bg_scr[...] = jnp.where(w_is_beta, jax.nn.sigmoid(ab), gcols)
    grow_scr[...] = -jnp.exp(ar_ref[0]) * _softplus(gr_ref[0, 0] + dtr_ref[0])
    o_scr[...] = jnp.zeros(o_scr.shape, F32)
    for d in range(2):
        if has_state:
            s_scr[d] = jnp.concatenate([s0_ref[0, d, 0], s0_ref[0, d, 1]], axis=0)
        else:
            s_scr[d] = jnp.zeros((2 * DK_C, DV_C), F32)

    ii = lax.broadcasted_iota(I32, (c_len, c4), 0)
    ll = lax.broadcasted_iota(I32, (c_len, c4), 1)
    jj = ll & (c_len - 1)
    lane_stream = ll >> lg
    fwd = lane_stream < 2
    one = lambda m: jnp.where(m, 1.0, 0.0)
    m_scr[0] = jnp.where(fwd, one(jj <= ii), one(jj >= ii))
    m_scr[1] = jnp.where(fwd, one(ii <= jj), one(ii >= jj))
    m_scr[2] = one((ii >> 3) == (jj >> 3))
    n_merge = lg - 3
    for t in range(n_merge):
        sh = 3 + t
        m_scr[3 + t] = jnp.where((ii >> (sh + 1)) == (jj >> (sh + 1)), one((ii >> sh) != (jj >> sh)), 0.0)
    eye = one(ii == jj)
    blk_scr[...] = one((lax.broadcasted_iota(I32, (c4, c4), 0) >> lg)
                       == (lax.broadcasted_iota(I32, (c4, c4), 1) >> lg)).astype(BF16)
    ci = lax.broadcasted_iota(I32, (c_len, c_len), 0)
    cj = lax.broadcasted_iota(I32, (c_len, c_len), 1)
    tri = (one(cj <= ci), one(cj >= ci))
    is_head0 = ((lax.broadcasted_iota(I32, (c4, DK_C), 0) >> lg) & 1) == 0

    def by_head(x):
        return jnp.concatenate([jnp.where(is_head0, x, 0.0), jnp.where(is_head0, 0.0, x)], axis=1).astype(BF16)

    def by_stream(cols):
        return jnp.where(fwd, jnp.where(lane_stream == 0, cols[0], cols[1]),
                         jnp.where(lane_stream == 2, cols[2], cols[3]))

    def block_diag(x):
        return jnp.concatenate([x.astype(BF16)] * 4, axis=0) * blk_scr[...]

    def tri_inverse(a_mats):
        a_d = [a * m_scr[2] for a in a_mats]
        a2 = [_dot(a.astype(BF16), block_diag(a)) for a in a_d]
        ps = [eye - a for a in a_d]
        both = [_dot(jnp.concatenate([p, x], axis=0).astype(BF16), block_diag(x)) for p, x in zip(ps, a2)]
        ps = [p + b[:c_len] for p, b in zip(ps, both)]
        ps = [p + _dot(p.astype(BF16), block_diag(b[c_len:])) for p, b in zip(ps, both)]
        for t in range(n_merge):
            zs = [_dot(p.astype(BF16), block_diag(a * m_scr[3 + t])) for p, a in zip(ps, a_mats)]
            ps = [p - _dot(z.astype(BF16), block_diag(p)) for p, z in zip(ps, zs)]
        return ps

    group = min(GDN_GROUP, nc)

    def prep_group(gi, carry):
        cs = [gi * group + t for t in range(group)]
        rows_c = [pl.ds(pl.multiple_of(c * c_len, c_len), c_len) for c in cs]
        out_rows = [pl.ds(pl.multiple_of(c * c4, c4), c4) for c in cs]
        kc = [k_scr[r, :] for r in rows_c]
        qc = [q_scr[r, :] for r in rows_c]
        kc4 = [jnp.concatenate([k] * 4, axis=0) for k in kc]
        qc4 = [jnp.concatenate([q] * 4, axis=0) for q in qc]
        gram = [_dot_nt(jnp.concatenate([k, q], axis=0), k4) for k, q, k4 in zip(kc, qc, kc4)]
        kk = [g[:c_len] for g in gram]
        qk = [g[c_len:] for g in gram]
        incl = m_scr[0]
        beta, gc_col, g_last, decay, a_mats = [], [], [], [], []
        for t in range(group):
            bg = bg_scr[rows_c[t], :]
            beta_s = [bg[:, 0:1], bg[:, 1:2], bg[:, 4:5], bg[:, 5:6]]
            gcol_s = [bg[:, 2:3], bg[:, 3:4], bg[:, 6:7], bg[:, 7:8]]
            grow = grow_scr[pl.ds(cs[t], 1), :]
            gc_s = [jnp.sum(tri[s // 2] * grow[:, s * c_len:(s + 1) * c_len], axis=1, keepdims=True)
                    for s in range(4)]
            gc_row = jnp.sum(m_scr[1] * by_stream(gcol_s), axis=0, keepdims=True)
            decay.append(jnp.exp((by_stream(gc_s) - gc_row) * incl) * incl)
            a_mats.append((by_stream(beta_s) * kk[t] * decay[t]) * (incl - eye))
            gl_s = [gc_s[s][c_len - 1:c_len, :] if s < 2 else gc_s[s][0:1, :] for s in range(4)]
            beta.append(jnp.concatenate(beta_s, axis=0))
            gc_col.append(jnp.concatenate(gc_s, axis=0))
            g_last.append(jnp.concatenate([jnp.broadcast_to(g, (c_len, 1)) for g in gl_s], axis=0))
        t_inv = tri_inverse(a_mats)
        blk2 = blk_scr[0:2 * c_len, 0:2 * c_len]
        for t in range(group):
            kf = kc4[t].astype(F32)
            qf = qc4[t].astype(F32)
            egc = jnp.exp(gc_col[t])
            vc = v_scr[rows_c[t], :]
            v4 = jnp.concatenate([vc[:, :DV_C], vc[:, DV_C:], vc[:, :DV_C], vc[:, DV_C:]], axis=0)
            kb = kf * (beta[t] * egc)
            rhs = jnp.concatenate([v4 * beta[t], kb], axis=1).astype(BF16)
            uw = _dot(block_diag(t_inv[t]), rhs)
            u_scr[out_rows[t], :] = uw[:, :DV_C]
            w = uw[:, DV_C:]
            w_scr[out_rows[t], :] = by_head(w)
            attn = (qk[t] * decay[t]).astype(BF16)
            at_scr[out_rows[t], :] = jnp.concatenate(
                [jnp.concatenate([attn[:, :2 * c_len]] * 2, axis=0) * blk2,
                 jnp.concatenate([attn[:, 2 * c_len:]] * 2, axis=0) * blk2], axis=0)
            qe = qf * egc
            qd_scr[out_rows[t], :] = by_head(qe)
            ke = kf * jnp.exp(g_last[t] - gc_col[t])
            kd_scr[out_rows[t], :] = by_head(ke)
            egl = jnp.exp(g_last[t])
            for s in range(4):
                eg_scr[pl.ds(cs[t] * 4 + s, 1), :] = jnp.broadcast_to(egl[s * c_len:s * c_len + 1, :], (1, DV_C))
        return carry

    lax.fori_loop(0, nc // group, prep_group, 0)

    def scan_chunk(c, carry):
        ccs = [c, nc - 1 - c]
        rows_d = [pl.ds(pl.multiple_of(ccs[d] * c4 + d * 2 * c_len, 2 * c_len), 2 * c_len) for d in range(2)]
        s_cat = [s_scr[d] for d in range(2)]
        sb = [s.astype(BF16) for s in s_cat]
        ws = [_dot(w_scr[rows_d[d], :], sb[d]) for d in range(2)]
        qs = [_dot(qd_scr[rows_d[d], :], sb[d]) for d in range(2)]
        vnb = [(u_scr[rows_d[d], :] - ws[d]).astype(BF16) for d in range(2)]
        kv = [_dot_tn(kd_scr[rows_d[d], :], vnb[d]) for d in range(2)]
        av = [_dot(at_scr[rows_d[d], :], vnb[d]) for d in range(2)]
        for d in range(2):
            e0 = jnp.broadcast_to(eg_scr[pl.ds(ccs[d] * 4 + 2 * d, 1), :], (DK_C, DV_C))
            e1 = jnp.broadcast_to(eg_scr[pl.ds(ccs[d] * 4 + 2 * d + 1, 1), :], (DK_C, DV_C))
            s_scr[d] = s_cat[d] * jnp.concatenate([e0, e1], axis=0) + kv[d]
            o = qs[d] + av[d]
            tok = pl.ds(pl.multiple_of(ccs[d] * c_len, c_len), c_len)
            o_scr[tok, 0:DV_C] += o[:c_len]
            o_scr[tok, DV_C:2 * DV_C] += o[c_len:]
        return carry

    lax.fori_loop(0, nc, scan_chunk, 0)

    for d in range(2):
        for r in range(2):
            st_ref[0, d, r] = s_scr[d, r * DK_C:(r + 1) * DK_C, :]
    ng = ng_ref[...]
    for s in range(0, n, rows):
        zg = zg_ref[s:s + rows, :].astype(F32)
        for r in range(2):
            o = o_scr[s:s + rows, r * DV_C:(r + 1) * DV_C]
            on = o * lax.rsqrt(jnp.mean(o * o, axis=-1, keepdims=True) + EPS) * ng
            o_ref[s:s + rows, r * DV_C:(r + 1) * DV_C] = (
                on * _silu(zg[:, r * DV_C:(r + 1) * DV_C])).astype(BF16)


def _gdn(z1, ab, bsz, n, conv_c, a_log, dt_bias, norm_g, s0):
    nc = n // GDN_C
    rows = min(n, 256)
    nqb = H_QK
    ab6 = ab.reshape(bsz, n, 2, 2, H_QK, 2)
    ab8 = ab6.transpose(0, 4, 1, 2, 3, 5).reshape(bsz, H_QK, n, 8)
    g_rows = ab6[:, :, :, 1].reshape(bsz, nc, GDN_C, 2, H_QK, 2).transpose(0, 4, 1, 3, 5, 2)
    g_rows = g_rows.reshape(bsz, H_QK, nc, 4 * GDN_C)

    def per_pair(p):
        p3 = p.astype(F32).reshape(2, H_QK, 2).transpose(1, 0, 2)
        return jnp.broadcast_to(p3[:, :, None, :], (H_QK, 2, 2, 2)).reshape(H_QK, 8)

    def per_stream(p):
        p3 = p.astype(F32).reshape(2, H_QK, 2).transpose(1, 0, 2).reshape(H_QK, 4)
        return jnp.repeat(p3, GDN_C, axis=1).reshape(H_QK, 1, 4 * GDN_C)

    a8, dt8 = per_pair(a_log), per_pair(dt_bias)
    in_specs = [pl.BlockSpec((n, DK_C), lambda b, j: (b, j)),
                pl.BlockSpec((n, DK_C), lambda b, j: (b, nqb + j)),
                pl.BlockSpec((n, 2 * DV_C), lambda b, j: (b, nqb + j)),
                pl.BlockSpec((n, 2 * DV_C), lambda b, j: (b, 2 * nqb + j)),
                pl.BlockSpec((4, DK_C), lambda b, j: (0, j)),
                pl.BlockSpec((4, DK_C), lambda b, j: (0, nqb + j)),
                pl.BlockSpec((4, 2 * DV_C), lambda b, j: (0, nqb + j)),
                pl.BlockSpec((1, 1, n, 8), lambda b, j: (b, j, 0, 0)),
                pl.BlockSpec((1, 1, nc, 4 * GDN_C), lambda b, j: (b, j, 0, 0)),
                pl.BlockSpec((1, 1, 8), lambda b, j: (j, 0, 0)),
                pl.BlockSpec((1, 1, 8), lambda b, j: (j, 0, 0)),
                pl.BlockSpec((1, 1, 4 * GDN_C), lambda b, j: (j, 0, 0)),
                pl.BlockSpec((1, 1, 4 * GDN_C), lambda b, j: (j, 0, 0)),
                pl.BlockSpec((1, DV_C), lambda b, j: (0, 0))]
    args = [z1, z1, z1, z1, conv_c, conv_c, conv_c, ab8, g_rows,
            a8.reshape(H_QK, 1, 8), dt8.reshape(H_QK, 1, 8),
            per_stream(a_log), per_stream(dt_bias), norm_g.reshape(1, DV_C)]
    if s0 is not None:
        in_specs.append(pl.BlockSpec((1, 2, 2, DK_C, DV_C), lambda b, j: (b, 0, j, 0, 0)))
        args.append(s0)
    return pl.pallas_call(
        functools.partial(_gdn_body, has_state=s0 is not None, rows=rows),
        grid=(bsz, H_QK), in_specs=in_specs,
        out_specs=[pl.BlockSpec((n, 2 * DV_C), lambda b, j: (b, j)),
                   pl.BlockSpec((1, 2, 2, DK_C, DV_C), lambda b, j: (b, 0, j, 0, 0))],
        out_shape=[jax.ShapeDtypeStruct((bsz * n, H_V * DV_C), BF16),
                   jax.ShapeDtypeStruct((bsz, 2, H_V, DK_C, DV_C), F32)],
        scratch_shapes=[pltpu.VMEM((n + 16, DK_C), F32), pltpu.VMEM((n + 16, 2 * DV_C), F32),
                        pltpu.VMEM((n, DK_C), BF16), pltpu.VMEM((n, DK_C), BF16),
                        pltpu.VMEM((n, 2 * DV_C), F32), pltpu.VMEM((n, 2 * DV_C), F32),
                        pltpu.VMEM((n, 8), F32), pltpu.VMEM((nc, 4 * GDN_C), F32),
                        pltpu.VMEM((2, 2 * DK_C, DV_C), F32),
                        pltpu.VMEM((GDN_C.bit_length() - 1, GDN_C, 4 * GDN_C), F32),
                        pltpu.VMEM((4 * GDN_C, 4 * GDN_C), BF16),
                        pltpu.VMEM((4 * n, DV_C), F32), pltpu.VMEM((4 * n, 2 * DK_C), BF16),
                        pltpu.VMEM((4 * n, 2 * GDN_C), BF16), pltpu.VMEM((4 * n, 2 * DK_C), BF16),
                        pltpu.VMEM((4 * n, 2 * DK_C), BF16), pltpu.VMEM((4 * nc, DV_C), F32)],
        compiler_params=_cparams(2), name="gated_deltanet")(*args)


def _router_body(x_ref, m_ref, g_ref, rwt_ref, h_ref, aff_ref):
    h = _adaln(x_ref[...], g_ref[...], m_ref[3:4, :], m_ref[4:5, :])
    h_ref[...] = h.astype(BF16)
    logits = _dot_nt(rwt_ref[...].astype(BF16), h.astype(BF16))
    ex = jnp.exp(logits - jnp.max(logits, axis=0, keepdims=True))
    aff_ref[0] = ex / jnp.sum(ex, axis=0, keepdims=True)


def _router(x, mod_l, row_fn, g, rw_t, bsz, n, tt):
    t, d = x.shape
    e = rw_t.shape[0]
    nt = n // tt
    return pl.pallas_call(
        _router_body, grid=(bsz, nt),
        in_specs=[pl.BlockSpec((tt, d), lambda b, i: (b * nt + i, 0)),
                  pl.BlockSpec((None, 6, d), lambda b, i: (row_fn(b), 0, 0)),
                  pl.BlockSpec((1, d), lambda b, i: (0, 0)),
                  pl.BlockSpec((e, d), lambda b, i: (0, 0))],
        out_specs=[pl.BlockSpec((tt, d), lambda b, i: (b * nt + i, 0)),
                   pl.BlockSpec((1, e, tt), lambda b, i: (b, 0, i))],
        out_shape=[jax.ShapeDtypeStruct((t, d), BF16), jax.ShapeDtypeStruct((bsz, e, n), F32)],
        compiler_params=_cparams(2), name="adaln_router")(x, mod_l, g.reshape(1, d), rw_t)


def _cumsum_lanes(m):
    r, n = m.shape
    blk = min(n, 256)
    tri = jnp.where(lax.broadcasted_iota(I32, (blk, blk), 0) <= lax.broadcasted_iota(I32, (blk, blk), 1),
                    1.0, 0.0).astype(BF16)
    outs = []
    run = jnp.zeros((r, 1), F32)
    for c in range(n // blk):
        loc = _dot(m[:, c * blk:(c + 1) * blk], tri) + run
        outs.append(loc)
        run = loc[:, blk - 1:blk]
    return outs[0] if len(outs) == 1 else jnp.concatenate(outs, axis=1)


def _topk_body(aff_ref, pos_ref, *, cap):
    bb, e, n = aff_ref.shape
    rows = bb * e
    bits = pltpu.bitcast(aff_ref[...].reshape(rows, n), I32)
    capf = float(cap)
    prefix = jnp.zeros((rows, 1), I32)
    for bit in range(30, -1, -1):
        cand = prefix | (1 << bit)
        cnt = jnp.sum(jnp.where(bits >= cand, 1.0, 0.0), axis=1, keepdims=True)
        prefix = jnp.where(cnt >= capf, cand, prefix)
    gt = bits > prefix
    eq = bits == prefix
    need = capf - jnp.sum(jnp.where(gt, 1.0, 0.0), axis=1, keepdims=True)
    eq_rank = _cumsum_lanes(jnp.where(eq, 1.0, 0.0).astype(BF16))
    sel = jnp.where(gt, 1.0, jnp.where(eq, jnp.where(eq_rank <= need, 1.0, 0.0), 0.0))
    slot = _cumsum_lanes(sel.astype(BF16)) - 1.0
    pos_ref[...] = jnp.where(sel > 0.5, slot, -1.0).astype(I32).reshape(bb, e, n)


def _topk(aff, cap):
    bsz, e, n = aff.shape
    return pl.pallas_call(
        functools.partial(_topk_body, cap=cap), grid=(1,),
        in_specs=[pl.BlockSpec((bsz, e, n), lambda i: (0, 0, 0))],
        out_specs=pl.BlockSpec((bsz, e, n), lambda i: (0, 0, 0)),
        out_shape=jax.ShapeDtypeStruct((bsz, e, n), I32),
        compiler_params=_cparams(1), name="expert_choice_topk")(aff)


def _gather_body(h_ref, pos_ref, aff_ref, xg_ref, gate_ref):
    eg, cap, _ = xg_ref.shape
    n = h_ref.shape[0]
    slot = lax.broadcasted_iota(I32, (cap, n), 0)
    for e in range(eg):
        hit = slot == pos_ref[0, e]
        onehot = jnp.where(hit, 1.0, 0.0).astype(BF16)
        xg_ref[e] = _dot(onehot, h_ref[...]).astype(BF16)
        gate_ref[e] = jnp.sum(jnp.where(hit, aff_ref[0, e], 0.0), axis=1, keepdims=True)


def _gather(h, pos4, aff4, bsz, n, cap, eg):
    d = h.shape[1]
    e = pos4.shape[1]
    return pl.pallas_call(
        _gather_body, grid=(bsz, e // eg),
        in_specs=[pl.BlockSpec((n, d), lambda b, g: (b, 0)),
                  pl.BlockSpec((1, eg, 1, n), lambda b, g: (b, g, 0, 0)),
                  pl.BlockSpec((1, eg, 1, n), lambda b, g: (b, g, 0, 0))],
        out_specs=[pl.BlockSpec((eg, cap, d), lambda b, g: (g, b, 0)),
                   pl.BlockSpec((eg, cap, 1), lambda b, g: (g, b, 0))],
        out_shape=[jax.ShapeDtypeStruct((e, bsz * cap, d), BF16),
                   jax.ShapeDtypeStruct((e, bsz * cap, 1), F32)],
        compiler_params=_cparams(2), name="moe_gather")(h, pos4, aff4)


def _ffn_body(x_ref, gate_ref, w1_ref, w3_ref, w2_ref, o_ref, hid, *, n_up):
    s = pl.program_id(2)

    @pl.when(s < n_up)
    def _():
        x = x_ref[0]
        a = _dot(x, w1_ref[0].astype(BF16))
        b = _dot(x, w3_ref[0].astype(BF16))
        hid[s] = (_silu(a) * b).astype(BF16)

    @pl.when(s >= n_up)
    def _():
        h = jnp.concatenate([hid[k] for k in range(n_up)], axis=1)
        o_ref[0] = (_dot(h, w2_ref[0].astype(BF16)) * gate_ref[0]).astype(BF16)


def _ffn(xg, gate, w1, w3, w2, layer, tm, tf):
    e, m, d = xg.shape
    ff = w1.shape[3]
    n_up = ff // tf
    up = lambda s: jnp.minimum(s, n_up - 1)
    down = lambda s: jnp.maximum(s - n_up, 0)
    return pl.pallas_call(
        functools.partial(_ffn_body, n_up=n_up), grid=(e, m // tm, n_up + d // tf),
        in_specs=[pl.BlockSpec((1, tm, d), lambda x, i, s: (x, i, 0)),
                  pl.BlockSpec((1, tm, 1), lambda x, i, s: (x, i, 0)),
                  pl.BlockSpec((None, 1, d, tf), lambda x, i, s: (layer, x, 0, up(s))),
                  pl.BlockSpec((None, 1, d, tf), lambda x, i, s: (layer, x, 0, up(s))),
                  pl.BlockSpec((None, 1, ff, tf), lambda x, i, s: (layer, x, 0, down(s)))],
        out_specs=pl.BlockSpec((1, tm, tf), lambda x, i, s: (x, i, down(s))),
        out_shape=jax.ShapeDtypeStruct((e, m, d), BF16),
        scratch_shapes=[pltpu.VMEM((n_up, tm, tf), BF16)],
        compiler_params=_cparams(3), name="expert_ffn")(xg, gate, w1, w3, w2)


def _combine_body(pos_ref, y_ref, x_ref, m_ref, fg_ref, o_ref, acc, *, final):
    e = pl.program_id(2)
    eg, cap, d = y_ref.shape
    tt = x_ref.shape[0]
    rows8 = [pos_ref[0, g] for g in range(eg)] + [jnp.zeros((1, tt), I32)] * ((-eg) % 8)
    pos_rows = jnp.concatenate(rows8, axis=0).astype(F32).astype(BF16)
    eye = jnp.where(lax.broadcasted_iota(I32, (tt, tt), 0) == lax.broadcasted_iota(I32, (tt, tt), 1),
                    1.0, 0.0).astype(BF16)
    pos_cols = _dot_nt(eye, pos_rows)
    slot = lax.broadcasted_iota(I32, (tt, cap), 1).astype(F32)
    onehot = jnp.concatenate([jnp.where(slot == pos_cols[:, g:g + 1], 1.0, 0.0).astype(BF16)
                              for g in range(eg)], axis=1)
    part = _dot(onehot, y_ref[...].reshape(eg * cap, d))

    @pl.when(e == 0)
    def _():
        acc[...] = part

    @pl.when(e > 0)
    def _():
        acc[...] += part

    @pl.when(e == pl.num_programs(2) - 1)
    def _():
        xn = x_ref[...] + m_ref[5:6, :] * acc[...]
        if final:
            xn = xn * lax.rsqrt(jnp.mean(xn * xn, axis=-1, keepdims=True) + EPS) * fg_ref[...]
        o_ref[...] = xn


def _combine(pos4, y, x, mod_l, row_fn, final_g, bsz, n, cap, tt, eg, final):
    t, d = x.shape
    e = y.shape[0]
    nt = n // tt
    return pl.pallas_call(
        functools.partial(_combine_body, final=final), grid=(bsz, nt, e // eg),
        in_specs=[pl.BlockSpec((1, eg, 1, tt), lambda b, i, x_: (b, x_, 0, i)),
                  pl.BlockSpec((eg, cap, d), lambda b, i, x_: (x_, b, 0)),
                  pl.BlockSpec((tt, d), lambda b, i, x_: (b * nt + i, 0)),
                  pl.BlockSpec((None, 6, d), lambda b, i, x_: (row_fn(b), 0, 0)),
                  pl.BlockSpec((1, d), lambda b, i, x_: (0, 0))],
        out_specs=pl.BlockSpec((tt, d), lambda b, i, x_: (b * nt + i, 0)),
        out_shape=jax.ShapeDtypeStruct((t, d), F32),
        scratch_shapes=[pltpu.VMEM((tt, d), F32)],
        compiler_params=_cparams(3), name="moe_combine")(pos4, y, x, mod_l, final_g.reshape(1, d))


def _ec_moe(x, mod_l, row_fn, norm_g, rw_t, w1, w3, w2, layer, final_g, bsz, n, final):
    e = rw_t.shape[0]
    cap = EC_FACTOR * n // e
    tt = min(n, 512)
    h, aff = _router(x, mod_l, row_fn, norm_g, rw_t, bsz, n, tt)
    pos = _topk(aff, cap)
    pos4 = pos.reshape(bsz, e, 1, n)
    eg = e if cap * e <= 1024 else 1
    xg, gate = _gather(h, pos4, aff.reshape(bsz, e, 1, n), bsz, n, cap, eg)
    m = bsz * cap
    tm = m if m <= 1024 else 1024
    y = _ffn(xg, gate, w1, w3, w2, layer, tm, 512)
    return _combine(pos4, y, x, mod_l, row_fn, final_g, bsz, n, cap, min(n, 256), e, final)


def _run_group(x3, mod, row_of_request, states, p, with_rope):
    bsz, n, d = x3.shape
    x = x3.reshape(bsz * n, d)
    tm = min(n, 1024) if row_of_request is not None else min(bsz * n, 1024)
    if row_of_request is None:
        tile_row = lambda i: 0
        req_row = lambda b: 0
    else:
        tiles_per_req = n // tm
        tile_row = lambda i: row_of_request + i // tiles_per_req
        req_row = lambda b: row_of_request + b
    s_lru, s_ret, s_gdn = states
    outs = {}
    depth = p["mod"].shape[0]
    for l in range(depth):
        mod_l = mod[l]
        last = l == depth - 1
        if l % 2 == 0:
            e = l // 2
            w_in = p["w_in0"][e]
            d_a = p["conv_a"].shape[-1]
            z_a = _inproj(x, mod_l, tile_row, p["norm1_g"][l], w_in, 0, 2 * d_a, F32, tm, 512, 0, 1)
            z_b = _inproj(x, mod_l, tile_row, p["norm1_g"][l], w_in, 2 * d_a, w_in.shape[1] - 2 * d_a,
                          ACT_DTYPE, tm, 512, 0, 1)
            h0 = jnp.zeros((bsz, 2, d_a), F32) if s_lru is None else s_lru[:, e]
            out_a, st_a = _rglru(z_a, bsz, n, p["conv_a"][e], p["lru_wa"][e], p["lru_wi"][e],
                                 p["lru_ba"][e], p["lru_bi"][e], p["lru_lam"][e], h0)
            out_b, st_b = _retention(z_b, bsz, n, p["ret_decay"][e], _rope_tables(n) if with_rope else None,
                                     None if s_ret is None else s_ret[:, e])
            outs.setdefault("lru", []).append(st_a)
            outs.setdefault("ret", []).append(st_b)
            x = _outproj([out_a, out_b], p["w_out0"][e], x, mod_l, tile_row, tm, 512, 2)
        else:
            o = l // 2
            w_in = p["w_in1"][o]
            nz = H_QK * DK_C * 2 + 2 * H_V * DV_C
            z1 = _inproj(x, mod_l, tile_row, p["norm1_g"][l], w_in, 0, nz, ACT_DTYPE, tm, 512, 0, 1)
            ab = _inproj(x, mod_l, tile_row, p["norm1_g"][l], w_in, nz, 4 * H_V, F32, tm, 4 * H_V, 0, 1)
            out_c, st_c = _gdn(z1, ab, bsz, n, p["conv_c"][o], p["gdn_a_log"][o], p["gdn_dt_bias"][o],
                               p["gdn_norm_g"][o], None if s_gdn is None else s_gdn[:, o])
            outs.setdefault("gdn", []).append(st_c)
            x = _outproj([out_c], p["w_out1"][o], x, mod_l, tile_row, tm, 512, 2)
        x = _ec_moe(x, mod_l, req_row, p["norm2_g"][l], p["router_t"][l], p["exp_w1"], p["exp_w3"],
                    p["exp_w2"], l, p["final_g"], bsz, n, last)
    return x.reshape(bsz, n, d), outs


def kernel(x_prompt, x_sample, state_rglru, state_ret, state_gdn, c, c_ctx, mod_w, mod_b, norm1_g, norm2_g, w_in0, conv_a, lru_wa, lru_ba, lru_wi, lru_bi, lru_lam, ret_decay, w_out0, w_in1, conv_c, gdn_a_log, gdn_dt_bias, gdn_norm_g, w_out1, router_w, exp_w1, exp_w3, exp_w2, final_g):
    pad = (-(c.shape[0] + 1)) % 8
    cond = jnp.concatenate([c_ctx[None, :], c, jnp.zeros((pad, c.shape[1]), F32)], axis=0)
    mod = _modulation(cond, mod_w, mod_b)
    p = dict(mod=mod, norm1_g=norm1_g, norm2_g=norm2_g,
             w_in0=w_in0.astype(BF16), conv_a=conv_a, lru_wa=lru_wa, lru_ba=lru_ba, lru_wi=lru_wi,
             lru_bi=lru_bi, lru_lam=lru_lam, ret_decay=ret_decay, w_out0=w_out0.astype(BF16),
             w_in1=w_in1.astype(BF16), conv_c=conv_c, gdn_a_log=gdn_a_log, gdn_dt_bias=gdn_dt_bias,
             gdn_norm_g=gdn_norm_g, w_out1=w_out1.astype(BF16), router_t=jnp.swapaxes(router_w, 1, 2),
             exp_w1=exp_w1, exp_w3=exp_w3, exp_w2=exp_w2, final_g=final_g)
    y_prompt, st = _run_group(x_prompt, mod, None, (None, None, None), p, False)
    y_sample, _ = _run_group(x_sample, mod, 1, (state_rglru, state_ret, state_gdn), p, True)
    return (y_prompt, y_sample, jnp.stack(st["lru"], axis=1), jnp.stack(st["ret"], axis=1),
            jnp.stack(st["gdn"], axis=1))
```
